```python
import math
import jax, jax.numpy as jnp
from jax import lax
import numpy as np

D_MODEL = 1024
BATCH = 8
SEQ = 4096
DEPTH = 4

PLE_DIM = 256
D_FF = 2816
N_MIXERS = 2
CONV_KERNEL = 31
HEAD_DIM = 64
N_HEADS = D_MODEL // HEAD_DIM
N_KV_GROUPS = 4
HEADS_PER_GROUP = N_HEADS // N_KV_GROUPS
Q_DIM = N_HEADS * HEAD_DIM
KV_DIM = N_KV_GROUPS * HEAD_DIM
NSA_IN = Q_DIM + 6 * KV_DIM + 3 * N_HEADS
CMP_LEN = 32
CMP_STRIDE = 16
CMP_HIDDEN = 256
SEL_BLOCK = 64
N_SELECT = 16
WINDOW = 512
N_BUCKETS = 32
MAX_EXACT = 16
MAX_DISTANCE = 2048
Q_BLOCK = 128
SEL_Q_CHUNK = 32
RMS_EPS = 1e-6
LN_EPS = 1e-5
FORCE_SCORE = 1e9

kernel_name = "hybrid_conformer_conv_nsa_macaron_trunk"


def rms_norm(x, g):
    xf = x.astype(jnp.float32)
    y = xf * lax.rsqrt(jnp.mean(xf * xf, axis=-1, keepdims=True) + RMS_EPS)
    return (y * g.astype(jnp.float32)).astype(x.dtype)


def swiglu(x, wg, wu, wd):
    return (jax.nn.silu(x @ wg) * (x @ wu)) @ wd


def t5_bucket(dist):
    n = jnp.maximum(dist, 0)
    nf = jnp.maximum(n, 1).astype(jnp.float32)
    large = MAX_EXACT + (jnp.log(nf / MAX_EXACT) / math.log(MAX_DISTANCE / MAX_EXACT)
                         * (N_BUCKETS - MAX_EXACT)).astype(jnp.int32)
    large = jnp.minimum(large, N_BUCKETS - 1)
    return jnp.where(n < MAX_EXACT, n, large)


def masked_softmax(s, mask):
    s = jnp.where(mask, s.astype(jnp.float32), -1e30)
    m = jnp.max(s, axis=-1, keepdims=True)
    e = jnp.exp(s - m) * mask
    return e / jnp.maximum(jnp.sum(e, axis=-1, keepdims=True), 1e-30)


def conformer_conv(h, w_pw1, b_pw1, w_dw, b_dw, ln_g, ln_b, w_pw2, b_pw2):
    u = h @ w_pw1 + b_pw1
    a, g = jnp.split(u, 2, axis=-1)
    u = a * jax.nn.sigmoid(g)
    u = lax.conv_general_dilated(
        u, w_dw[:, None, :].astype(u.dtype), window_strides=(1,),
        padding=[(CONV_KERNEL - 1, 0)], dimension_numbers=('NWC', 'WIO', 'NWC'),
        feature_group_count=D_MODEL) + b_dw
    uf = u.astype(jnp.float32)
    mu = jnp.mean(uf, axis=-1, keepdims=True)
    var = jnp.mean(jnp.square(uf - mu), axis=-1, keepdims=True)
    uf = (uf - mu) * lax.rsqrt(var + LN_EPS) * ln_g.astype(jnp.float32) + ln_b.astype(jnp.float32)
    u = jax.nn.silu(uf).astype(h.dtype)
    return u @ w_pw2 + b_pw2


def compress_blocks(k, pos, w1, w2):
    B, S = k.shape[0], k.shape[1]
    n_cmp = (S - CMP_LEN) // CMP_STRIDE + 1
    idx = jnp.arange(n_cmp)[:, None] * CMP_STRIDE + jnp.arange(CMP_LEN)[None, :]
    blk = k[:, idx] + pos[None, None, :, None, :]
    blk = blk.transpose(0, 1, 3, 2, 4).reshape(B, n_cmp, N_KV_GROUPS, CMP_LEN * HEAD_DIM)
    return jax.nn.gelu(blk @ w1) @ w2


def nsa_mixer(h, w_in, pos_k, pos_v, wk1, wk2, wv1, wv2, w_out, rel_bias):
    B, S, _ = h.shape
    G, R = N_KV_GROUPS, HEADS_PER_GROUP
    scale = HEAD_DIM ** -0.5
    proj = h @ w_in
    q = proj[..., :Q_DIM].reshape(B, S, G, R, HEAD_DIM)
    kvs = proj[..., Q_DIM:Q_DIM + 6 * KV_DIM].reshape(B, S, 6, G, HEAD_DIM)
    k_c, v_c, k_s, v_s, k_w, v_w = (kvs[:, :, j] for j in range(6))
    gates = jax.nn.sigmoid(proj[..., Q_DIM + 6 * KV_DIM:].astype(jnp.float32))
    gates = gates.reshape(B, S, G, R, 3).astype(h.dtype)

    n_qb = S // Q_BLOCK
    n_cmp = (S - CMP_LEN) // CMP_STRIDE + 1
    n_sel = S // SEL_BLOCK
    k_sel = min(N_SELECT, n_sel)
    q_blocks = q.reshape(B, n_qb, Q_BLOCK, G, R, HEAD_DIM).transpose(1, 0, 2, 3, 4, 5)
    qb_ids = jnp.arange(n_qb)

    kc = compress_blocks(k_c, pos_k, wk1, wk2)
    vc = compress_blocks(v_c, pos_v, wv1, wv2)
    cmp_start = jnp.arange(n_cmp) * CMP_STRIDE
    cmp_end = cmp_start + CMP_LEN - 1
    sel_start = jnp.arange(n_sel) * SEL_BLOCK
    overlap = ((cmp_start[:, None] < sel_start[None, :] + SEL_BLOCK)
               & (cmp_start[:, None] + CMP_LEN > sel_start[None, :])).astype(jnp.float32)

    def cmp_step(args):
        qb, bid = args
        t = bid * Q_BLOCK + jnp.arange(Q_BLOCK)
        dist = t[:, None] - cmp_end[None, :]
        bias = rel_bias[t5_bucket(dist)].reshape(Q_BLOCK, n_cmp, G, R).transpose(2, 3, 0, 1)
        s = jnp.einsum('bqgrd,bcgd->bgrqc', qb, kc).astype(jnp.float32) * scale + bias.astype(jnp.float32)
        pr = masked_softmax(s, dist >= 0)
        o = jnp.einsum('bgrqc,bcgd->bqgrd', pr.astype(vc.dtype), vc)
        imp = jnp.einsum('bgrqc,cn->bgqn', pr, overlap)
        j = jnp.arange(n_sel)[None, :]
        cur = (t // SEL_BLOCK)[:, None]
        valid = j * SEL_BLOCK <= t[:, None]
        forced = (j == 0) | (j == cur) | (j == cur - 1)
        score = jnp.where(forced, FORCE_SCORE, jnp.where(valid, imp, -1.0))
        _, sel = lax.top_k(score, k_sel)
        return o, sel.astype(jnp.int32)

    o_c, sel_idx = lax.map(cmp_step, (q_blocks, qb_ids))
    o_c = o_c.transpose(1, 0, 2, 3, 4, 5).reshape(B, S, G, R, HEAD_DIM)
    sel_idx = sel_idx.transpose(1, 2, 0, 3, 4).reshape(B, G, S, k_sel)

    kb = k_s.reshape(B, n_sel, SEL_BLOCK, G, HEAD_DIM).transpose(0, 3, 1, 2, 4)
    vb = v_s.reshape(B, n_sel, SEL_BLOCK, G, HEAD_DIM).transpose(0, 3, 1, 2, 4)
    n_qs = S // SEL_Q_CHUNK
    q_chunks = q.reshape(B, n_qs, SEL_Q_CHUNK, G, R, HEAD_DIM).transpose(1, 0, 2, 3, 4, 5)
    idx_chunks = sel_idx.reshape(B, G, n_qs, SEL_Q_CHUNK, k_sel).transpose(2, 0, 1, 3, 4)
    table_g = rel_bias.reshape(N_BUCKETS, G, R).transpose(1, 0, 2)
    gather = jax.vmap(jax.vmap(lambda blocks, ids: blocks[ids]))
    n_keys = k_sel * SEL_BLOCK

    def sel_step(args):
        qc, ic, cid = args
        ks = gather(kb, ic).reshape(B, G, SEL_Q_CHUNK, n_keys, HEAD_DIM)
        vs = gather(vb, ic).reshape(B, G, SEL_Q_CHUNK, n_keys, HEAD_DIM)
        kpos = (ic[..., None] * SEL_BLOCK + jnp.arange(SEL_BLOCK)).reshape(B, G, SEL_Q_CHUNK, n_keys)
        t = cid * SEL_Q_CHUNK + jnp.arange(SEL_Q_CHUNK)
        dist = t[None, None, :, None] - kpos
        bias = jax.vmap(lambda tb, bk: tb[bk], in_axes=(0, 1), out_axes=1)(table_g, t5_bucket(dist))
        bias = bias.transpose(0, 1, 4, 2, 3)
        s = jnp.einsum('bqgrd,bgqnd->bgrqn', qc, ks).astype(jnp.float32) * scale + bias.astype(jnp.float32)
        pr = masked_softmax(s, (dist >= 0)[:, :, None])
        return jnp.einsum('bgrqn,bgqnd->bqgrd', pr.astype(vs.dtype), vs)

    o_s = lax.map(sel_step, (q_chunks, idx_chunks, jnp.arange(n_qs)))
    o_s = o_s.transpose(1, 0, 2, 3, 4, 5).reshape(B, S, G, R, HEAD_DIM)

    kw_pad = jnp.pad(k_w, ((0, 0), (WINDOW, 0), (0, 0), (0, 0)))
    vw_pad = jnp.pad(v_w, ((0, 0), (WINDOW, 0), (0, 0), (0, 0)))
    slab = Q_BLOCK + WINDOW

    def win_step(args):
        qb, bid = args
        s0 = bid * Q_BLOCK
        ks = lax.dynamic_slice_in_dim(kw_pad, s0, slab, axis=1)
        vs = lax.dynamic_slice_in_dim(vw_pad, s0, slab, axis=1)
        t = s0 + jnp.arange(Q_BLOCK)
        kpos = s0 - WINDOW + jnp.arange(slab)
        dist = t[:, None] - kpos[None, :]
        mask = (dist >= 0) & (dist < WINDOW) & (kpos[None, :] >= 0)
        bias = rel_bias[t5_bucket(dist)].reshape(Q_BLOCK, slab, G, R).transpose(2, 3, 0, 1)
        s = jnp.einsum('bqgrd,bkgd->bgrqk', qb, ks).astype(jnp.float32) * scale + bias.astype(jnp.float32)
        pr = masked_softmax(s, mask)
        return jnp.einsum('bgrqk,bkgd->bqgrd', pr.astype(vs.dtype), vs)

    o_w = lax.map(win_step, (q_blocks, qb_ids))
    o_w = o_w.transpose(1, 0, 2, 3, 4, 5).reshape(B, S, G, R, HEAD_DIM)

    o = gates[..., 0, None] * o_c + gates[..., 1, None] * o_s + gates[..., 2, None] * o_w
    return o.reshape(B, S, Q_DIM) @ w_out


def setup_inputs(seed: int = 0) -> dict:
    key = jax.random.key(seed)
    ks = iter(jax.random.split(key, 48))
    n_conv = (DEPTH + 1) // 2
    n_nsa = DEPTH // 2
    f32 = jnp.float32

    def w(shape, fan_in):
        return jax.random.normal(next(ks), shape, f32) * (fan_in ** -0.5)

    def gain(shape):
        return 1.0 + 0.02 * jax.random.normal(next(ks), shape, f32)

    def small(shape):
        return 0.02 * jax.random.normal(next(ks), shape, f32)

    return {
        "x": jax.random.normal(next(ks), (BATCH, SEQ, D_MODEL), f32),
        "p": jax.random.normal(next(ks), (DEPTH, BATCH, SEQ, PLE_DIM), f32),
        "rel_bias": 0.5 * jax.random.normal(next(ks), (N_BUCKETS, N_HEADS), f32),
        "ffn1_norm": gain((DEPTH, D_MODEL)),
        "ffn1_w_gate": w((DEPTH, D_MODEL, D_FF), D_MODEL),
        "ffn1_w_up": w((DEPTH, D_MODEL, D_FF), D_MODEL),
        "ffn1_w_down": w((DEPTH, D_FF, D_MODEL), D_FF),
        "mix_norm": gain((DEPTH, D_MODEL)),
        "ffn2_norm": gain((DEPTH, D_MODEL)),
        "ffn2_w_gate": w((DEPTH, D_MODEL, D_FF), D_MODEL),
        "ffn2_w_up": w((DEPTH, D_MODEL, D_FF), D_MODEL),
        "ffn2_w_down": w((DEPTH, D_FF, D_MODEL), D_FF),
        "ple_norm": gain((DEPTH, D_MODEL)),
        "ple_w_gate": w((DEPTH, D_MODEL, D_MODEL), D_MODEL),
        "ple_w_in": w((DEPTH, PLE_DIM, D_MODEL), PLE_DIM),
        "conv_w_pw1": w((n_conv, D_MODEL, 2 * D_MODEL), D_MODEL),
        "conv_b_pw1": small((n_conv, 2 * D_MODEL)),
        "conv_w_dw": w((n_conv, CONV_KERNEL, D_MODEL), CONV_KERNEL),
        "conv_b_dw": small((n_conv, D_MODEL)),
        "conv_ln_g": gain((n_conv, D_MODEL)),
        "conv_ln_b": small((n_conv, D_MODEL)),
        "conv_w_pw2": w((n_conv, D_MODEL, D_MODEL), D_MODEL),
        "conv_b_pw2": small((n_conv, D_MODEL)),
        "nsa_w_in": w((n_nsa, D_MODEL, NSA_IN), D_MODEL),
        "nsa_cmp_pos_k": small((n_nsa, CMP_LEN, HEAD_DIM)),
        "nsa_cmp_pos_v": small((n_nsa, CMP_LEN, HEAD_DIM)),
        "nsa_cmp_wk1": w((n_nsa, CMP_LEN * HEAD_DIM, CMP_HIDDEN), CMP_LEN * HEAD_DIM),
        "nsa_cmp_wk2": w((n_nsa, CMP_HIDDEN, HEAD_DIM), CMP_HIDDEN),
        "nsa_cmp_wv1": w((n_nsa, CMP_LEN * HEAD_DIM, CMP_HIDDEN), CMP_LEN * HEAD_DIM),
        "nsa_cmp_wv2": w((n_nsa, CMP_HIDDEN, HEAD_DIM), CMP_HIDDEN),
        "nsa_w_out": w((n_nsa, Q_DIM, D_MODEL), Q_DIM),
        "final_norm": gain((D_MODEL,)),
    }


def reference(x, p, rel_bias, ffn1_norm, ffn1_w_gate, ffn1_w_up, ffn1_w_down, mix_norm,
              ffn2_norm, ffn2_w_gate, ffn2_w_up, ffn2_w_down, ple_norm, ple_w_gate, ple_w_in,
              conv_w_pw1, conv_b_pw1, conv_w_dw, conv_b_dw, conv_ln_g, conv_ln_b, conv_w_pw2, conv_b_pw2,
              nsa_w_in, nsa_cmp_pos_k, nsa_cmp_pos_v, nsa_cmp_wk1, nsa_cmp_wk2, nsa_cmp_wv1, nsa_cmp_wv2,
              nsa_w_out, final_norm):
    for i in range(DEPTH):
        x = x + 0.5 * swiglu(rms_norm(x, ffn1_norm[i]), ffn1_w_gate[i], ffn1_w_up[i], ffn1_w_down[i])
        h = rms_norm(x, mix_norm[i])
        j = i // N_MIXERS
        if i % N_MIXERS == 0:
            x = x + conformer_conv(h, conv_w_pw1[j], conv_b_pw1[j], conv_w_dw[j], conv_b_dw[j],
                                   conv_ln_g[j], conv_ln_b[j], conv_w_pw2[j], conv_b_pw2[j])
        else:
            x = x + nsa_mixer(h, nsa_w_in[j], nsa_cmp_pos_k[j], nsa_cmp_pos_v[j], nsa_cmp_wk1[j],
                              nsa_cmp_wk2[j], nsa_cmp_wv1[j], nsa_cmp_wv2[j], nsa_w_out[j], rel_bias)
        x = x + 0.5 * swiglu(rms_norm(x, ffn2_norm[i]), ffn2_w_gate[i], ffn2_w_up[i], ffn2_w_down[i])
        gate = jax.nn.sigmoid(rms_norm(x, ple_norm[i]) @ ple_w_gate[i])
        x = x + gate * (p[i] @ ple_w_in[i])
    return rms_norm(x, final_norm)
```

```python
import functools
import math

import jax
import jax.numpy as jnp
from jax import lax
from jax.experimental import pallas as pl
from jax.experimental.pallas import tpu as pltpu

F32 = jnp.float32
BF16 = jnp.bfloat16

D_MODEL = 1024
D_FF = 2816
CONV_KERNEL = 31
HEAD_DIM = 64
N_HEADS = 16
N_GROUPS = 4
HEADS_PER_GROUP = 4
GROUP_DIM = HEADS_PER_GROUP * HEAD_DIM
KV_DIM = N_GROUPS * HEAD_DIM
CMP_LEN = 32
CMP_STRIDE = 16
SEL_BLOCK = 64
N_SELECT = 16
WINDOW = 512
N_BUCKETS = 32
MAX_EXACT = 16
MAX_DISTANCE = 2048
RMS_EPS = 1e-6
LN_EPS = 1e-5
FORCE_SCORE = 1e9

LANES = 128
QT = 128
KT = 256
HALO = 32
MASKED = -1e30
MASK_TERM = -(2.0 ** 100)
N_FAR = MAX_DISTANCE // LANES + 1
N_WIN_BLOCKS = WINDOW // LANES + 1
VMEM_LIMIT = 56 * 1024 * 1024


def _params(n_axes):
    return pltpu.CompilerParams(dimension_semantics=("arbitrary",) * n_axes,
                                vmem_limit_bytes=VMEM_LIMIT)


def _resident(shape):
    nd = len(shape)
    return pl.BlockSpec(shape, lambda *_: (0,) * nd, pipeline_mode=pl.Buffered(1))


def _rms(x, g):
    return x * lax.rsqrt(jnp.mean(x * x, axis=-1, keepdims=True) + RMS_EPS) * g


def _dot(a, b):
    return jnp.dot(a, b, preferred_element_type=F32)


def _dot_nt(a, b):
    return lax.dot_general(a, b, (((1,), (1,)), ((), ())), preferred_element_type=F32)


def _ffn_kernel(x_ref, g_ref, wg_ref, wu_ref, wd_ref, o_ref, *, ff_chunk):
    x = x_ref[...]
    xn = _rms(x, g_ref[...]).astype(BF16)
    acc = jnp.zeros_like(x)
    for c in range(D_FF // ff_chunk):
        sl = slice(c * ff_chunk, (c + 1) * ff_chunk)
        gate = _dot(xn, wg_ref[:, sl])
        up = _dot(xn, wu_ref[:, sl])
        h = (gate * jax.nn.sigmoid(gate)) * up
        acc = acc + _dot(h.astype(BF16), wd_ref[sl, :])
    o_ref[...] = x + 0.5 * acc


def _ffn(x2, norm_g, wg, wu, wd, *, tm=512, ff_chunk=1408):
    t, d = x2.shape
    tile = pl.BlockSpec((tm, d), lambda i: (i, 0))
    return pl.pallas_call(
        functools.partial(_ffn_kernel, ff_chunk=ff_chunk),
        out_shape=jax.ShapeDtypeStruct((t, d), F32),
        grid=(t // tm,),
        in_specs=[tile, _resident((1, d)), _resident((d, D_FF)), _resident((d, D_FF)), _resident((D_FF, d))],
        out_specs=tile,
        compiler_params=_params(1),
        name="ffn",
    )(x2, norm_g, wg, wu, wd)


def _ple_kernel(x_ref, p_ref, g_ref, wgate_ref, win_ref, fin_ref, o_ref, *, final):
    x = x_ref[...]
    gate = jax.nn.sigmoid(_dot(_rms(x, g_ref[...]).astype(BF16), wgate_ref[...]))
    y = x + gate * _dot(p_ref[...].astype(BF16), win_ref[...])
    if final:
        y = _rms(y, fin_ref[...])
    o_ref[...] = y


def _ple(x2, p2, norm_g, w_gate, w_in, final_g, *, final, tm=512):
    t, d = x2.shape
    pd = p2.shape[1]
    tile = pl.BlockSpec((tm, d), lambda i: (i, 0))
    return pl.pallas_call(
        functools.partial(_ple_kernel, final=final),
        out_shape=jax.ShapeDtypeStruct((t, d), F32),
        grid=(t // tm,),
        in_specs=[tile, pl.BlockSpec((tm, pd), lambda i: (i, 0)), _resident((1, d)),
                  _resident((d, d)), _resident((pd, d)), _resident((1, d))],
        out_specs=tile,
        compiler_params=_params(1),
        name="ple",
    )(x2, p2, norm_g, w_gate, w_in, final_g)


def _conv_in_kernel(x_ref, g_ref, w_ref, b_ref, o_ref):
    hn = _rms(x_ref[...], g_ref[...]).astype(BF16)
    d = o_ref.shape[-1]
    a = _dot(hn, w_ref[:, :d]) + b_ref[:, :d]
    gt = _dot(hn, w_ref[:, d:]) + b_ref[:, d:]
    o_ref[...] = a * jax.nn.sigmoid(gt)


def _conv_in(x2, norm_g, w_pw1, b_pw1, *, tm=512):
    t, d = x2.shape
    tile = pl.BlockSpec((tm, d), lambda i: (i, 0))
    return pl.pallas_call(
        _conv_in_kernel,
        out_shape=jax.ShapeDtypeStruct((t, d), F32),
        grid=(t // tm,),
        in_specs=[tile, _resident((1, d)), _resident((d, 2 * d)), _resident((1, 2 * d))],
        out_specs=tile,
        compiler_params=_params(1),
        name="conv_in",
    )(x2, norm_g, w_pw1, b_pw1)


def _conv_out_kernel(u_ref, halo_ref, x_ref, wdw_ref, bdw_ref, lng_ref, lnb_ref, w2_ref, b2_ref,
                     o_ref, pad_ref, conv_ref, *, ts):
    first = pl.program_id(1) == 0
    pad_ref[0:HALO, :] = jnp.where(first, 0.0, halo_ref[0])
    pad_ref[HALO:, :] = u_ref[0]
    off = HALO - (CONV_KERNEL - 1)
    d = u_ref.shape[-1]
    for c in range(d // LANES):
        cs = slice(c * LANES, (c + 1) * LANES)
        acc = jnp.zeros((ts, LANES), F32)
        for k in range(CONV_KERNEL):
            acc = acc + wdw_ref[k:k + 1, cs] * pad_ref[off + k:off + k + ts, cs]
        conv_ref[:, cs] = acc + bdw_ref[:, cs]
    u = conv_ref[...]
    mu = jnp.mean(u, axis=-1, keepdims=True)
    var = jnp.mean(jnp.square(u - mu), axis=-1, keepdims=True)
    y = (u - mu) * lax.rsqrt(var + LN_EPS) * lng_ref[...] + lnb_ref[...]
    y = y * jax.nn.sigmoid(y)
    o_ref[0] = x_ref[0] + _dot(y.astype(BF16), w2_ref[...]) + b2_ref[...]


def _conv_out(u3, x3, w_dw, b_dw, ln_g, ln_b, w_pw2, b_pw2, *, ts=256):
    b, s, d = u3.shape
    tile = pl.BlockSpec((1, ts, d), lambda bi, i: (bi, i, 0))
    halo = pl.BlockSpec((1, HALO, d), lambda bi, i: (bi, jnp.maximum(i * (ts // HALO) - 1, 0), 0))
    return pl.pallas_call(
        functools.partial(_conv_out_kernel, ts=ts),
        out_shape=jax.ShapeDtypeStruct((b, s, d), F32),
        grid=(b, s // ts),
        in_specs=[tile, halo, tile, _resident((CONV_KERNEL, d)), _resident((1, d)), _resident((1, d)),
                  _resident((1, d)), _resident((d, d)), _resident((1, d))],
        out_specs=tile,
        scratch_shapes=[pltpu.VMEM((ts + HALO, d), F32), pltpu.VMEM((ts, d), F32)],
        compiler_params=_params(2),
        name="conv_out",
    )(u3, u3, x3, w_dw, b_dw, ln_g, ln_b, w_pw2, b_pw2)


def _nsa_in_kernel(x_ref, g_ref, wq_ref, wkv_ref, wgt_ref,
                   q_ref, kc_ref, vc_ref, ksa_ref, vs_ref, kw_ref, vw_ref, gt_ref, *, tm):
    hn = _rms(x_ref[0], g_ref[...]).astype(BF16)
    q_ref[0] = (_dot(hn, wq_ref[...]) * (HEAD_DIM ** -0.5)).astype(BF16)
    gt_ref[0] = jax.nn.sigmoid(_dot(hn, wgt_ref[...]))
    kv = _dot(hn, wkv_ref[...])
    pos = pl.program_id(1) * tm + lax.broadcasted_iota(jnp.int32, (tm, SEL_BLOCK), 0)
    blk = lax.broadcasted_iota(jnp.int32, (tm, SEL_BLOCK), 1)
    onehot = jnp.where(pos // SEL_BLOCK == blk, 1.0, 0.0).astype(BF16)
    for g in range(N_GROUPS):
        def part(j):
            lo = j * KV_DIM + g * HEAD_DIM
            return kv[:, lo:lo + HEAD_DIM]
        kc_ref[0, g] = part(0)
        vc_ref[0, g] = part(1)
        ksa_ref[0, g] = jnp.concatenate([part(2).astype(BF16), onehot], axis=1)
        vs_ref[0, g] = part(3).astype(BF16)
        kw_ref[0, g] = part(4).astype(BF16)
        vw_ref[0, g] = part(5).astype(BF16)


def _nsa_in(x3, norm_g, wq, wkv, wgt, *, tm=512):
    b, s, d = x3.shape
    per_group = lambda width: pl.BlockSpec((1, N_GROUPS, tm, width), lambda bi, i: (bi, 0, i, 0))
    rows = lambda width: pl.BlockSpec((1, tm, width), lambda bi, i: (bi, i, 0))
    kv_shape = lambda width, dt: jax.ShapeDtypeStruct((b, N_GROUPS, s, width), dt)
    return pl.pallas_call(
        functools.partial(_nsa_in_kernel, tm=tm),
        out_shape=(jax.ShapeDtypeStruct((b, s, d), BF16),
                   kv_shape(HEAD_DIM, F32), kv_shape(HEAD_DIM, F32),
                   kv_shape(2 * HEAD_DIM, BF16), kv_shape(HEAD_DIM, BF16),
                   kv_shape(HEAD_DIM, BF16), kv_shape(HEAD_DIM, BF16),
                   jax.ShapeDtypeStruct((b, s, N_GROUPS * LANES), F32)),
        grid=(b, s // tm),
        in_specs=[rows(d), _resident((1, d)), _resident(wq.shape), _resident(wkv.shape), _resident(wgt.shape)],
        out_specs=(rows(d), per_group(HEAD_DIM), per_group(HEAD_DIM), per_group(2 * HEAD_DIM),
                   per_group(HEAD_DIM), per_group(HEAD_DIM), per_group(HEAD_DIM), rows(N_GROUPS * LANES)),
        compiler_params=_params(2),
        name="nsa_in",
    )(x3, norm_g, wq, wkv, wgt)


def _compress_kernel(kx_ref, vx_ref, pk_ref, pv_ref, wk1_ref, wk2_ref, wv1_ref, wv2_ref, kc_ref, vc_ref):
    def mlp(x_ref, p_ref, w1_ref, w2_ref):
        x = x_ref[0, 0]
        nc = x.shape[0]
        first = _dot((x + p_ref[0:1, :]).astype(BF16), w1_ref[0])
        second = _dot((x + p_ref[1:2, :]).astype(BF16), w1_ref[1])
        hidden = first + pltpu.roll(second, nc - 1, 0)
        return _dot(jax.nn.gelu(hidden).astype(BF16), w2_ref[...])
    kc_ref[0, 0] = mlp(kx_ref, pk_ref, wk1_ref, wk2_ref).astype(BF16)
    vc_ref[0, 0] = mlp(vx_ref, pv_ref, wv1_ref, wv2_ref).astype(BF16)


def _compress(kx, vx, pos_k, pos_v, wk1, wk2, wv1, wv2):
    b, g, nc, width = kx.shape
    blk = pl.BlockSpec((1, 1, nc, width), lambda bi, gi: (bi, gi, 0, 0))
    out = pl.BlockSpec((1, 1, nc, HEAD_DIM), lambda bi, gi: (bi, gi, 0, 0))
    return pl.pallas_call(
        _compress_kernel,
        out_shape=(jax.ShapeDtypeStruct((b, g, nc, HEAD_DIM), BF16),) * 2,
        grid=(b, g),
        in_specs=[blk, blk, _resident(pos_k.shape), _resident(pos_v.shape), _resident(wk1.shape),
                  _resident(wk2.shape), _resident(wv1.shape), _resident(wv2.shape)],
        out_specs=(out, out),
        compiler_params=_params(2),
        name="compress",
    )(kx, vx, pos_k, pos_v, wk1, wk2, wv1, wv2)


def _t5_bucket(dist):
    n = jnp.maximum(dist, 0)
    nf = jnp.maximum(n, 1).astype(F32)
    large = MAX_EXACT + (jnp.log(nf / MAX_EXACT) / math.log(MAX_DISTANCE / MAX_EXACT)
                         * (N_BUCKETS - MAX_EXACT)).astype(jnp.int32)
    large = jnp.minimum(large, N_BUCKETS - 1)
    return jnp.where(n < MAX_EXACT, n, large)


def _bias_table_kernel(rb_ref, bucket_ref, o_ref):
    h = pl.program_id(0)
    bucket = bucket_ref[...]
    out = jnp.full(bucket.shape, MASKED, F32)
    for bkt in range(N_BUCKETS):
        out = jnp.where(bucket == bkt, rb_ref[bkt, h], out)
    o_ref[0] = out


def _bias_table(rel_bias, bucket, *, tr):
    rows, lanes = bucket.shape
    return pl.pallas_call(
        _bias_table_kernel,
        out_shape=jax.ShapeDtypeStruct((N_HEADS, rows, lanes), F32),
        grid=(N_HEADS, rows // tr),
        in_specs=[pl.BlockSpec(memory_space=pltpu.SMEM), pl.BlockSpec((tr, lanes), lambda h, i: (i, 0))],
        out_specs=pl.BlockSpec((1, tr, lanes), lambda h, i: (h, i, 0)),
        compiler_params=_params(2),
        name="bias_table",
    )(rel_bias, bucket)


def _bias_tables(rel_bias, s):
    nc = s // CMP_STRIDE
    i = jnp.arange(LANES)[:, None]
    l = jnp.arange(LANES)[None, :]

    def toeplitz_blocks(offsets, max_dist):
        dist = LANES * offsets[:, None, None] + i[None] - l[None]
        valid = (dist >= 0) & (dist < max_dist)
        return jnp.where(valid, _t5_bucket(dist), -1).astype(jnp.int32).reshape(-1, LANES)

    sel_bucket = toeplitz_blocks(jnp.arange(-1, N_FAR + 1), 1 << 30)
    win_bucket = toeplitz_blocks(jnp.arange(-1, N_WIN_BLOCKS), WINDOW)
    t = jnp.arange(s)[:, None]
    cmp_dist = t - (jnp.arange(nc)[None, :] * CMP_STRIDE + CMP_LEN - 1)
    cmp_bucket = jnp.where(cmp_dist >= 0, _t5_bucket(cmp_dist), -1).astype(jnp.int32)
    t_sel = _bias_table(rel_bias, sel_bucket, tr=LANES).reshape(N_HEADS, N_FAR + 2, LANES, LANES)
    t_win = _bias_table(rel_bias, win_bucket, tr=LANES).reshape(N_HEADS, N_WIN_BLOCKS + 1, LANES, LANES)
    t_cmp = _bias_table(rel_bias, cmp_bucket, tr=512)
    return t_sel, t_win, t_cmp


def _cmp_kernel(q_ref, kc_ref, vc_ref, tc_ref, ov_ref, oc_ref, mt_ref, *, n_sel, k_sel):
    q = q_ref[0]
    kc = kc_ref[0, 0]
    vc = vc_ref[0, 0]
    outs = []
    psum = None
    for r in range(HEADS_PER_GROUP):
        bias = tc_ref[r]
        s = _dot_nt(q[:, r * HEAD_DIM:(r + 1) * HEAD_DIM], kc) + bias
        m = jnp.max(s, axis=-1, keepdims=True)
        e = jnp.where(bias > 0.5 * MASKED, jnp.exp(s - m), 0.0)
        pr = e / jnp.maximum(jnp.sum(e, axis=-1, keepdims=True), 1e-30)
        outs.append(_dot(pr.astype(BF16), vc))
        psum = pr if psum is None else psum + pr
    oc_ref[0] = jnp.concatenate(outs, axis=1)

    hi = psum.astype(BF16)
    rem = psum - hi.astype(F32)
    mid = rem.astype(BF16)
    lo = (rem - mid.astype(F32)).astype(BF16)
    ov = ov_ref[...]
    imp = _dot(hi, ov) + _dot(mid, ov) + _dot(lo, ov)

    t = pl.program_id(0) * QT + lax.broadcasted_iota(jnp.int32, (QT, LANES), 0)
    j = lax.broadcasted_iota(jnp.int32, (QT, LANES), 1)
    cur = t // SEL_BLOCK
    forced = (j == 0) | (j == cur) | (j == cur - 1)
    score = jnp.where(forced, FORCE_SCORE, jnp.where(j * SEL_BLOCK <= t, imp, -1.0))
    score_t = score.T[:n_sel]

    sub = 8
    ranks = []
    for v in range(n_sel // sub):
        mine = score_t[v * sub:(v + 1) * sub]
        jj = v * sub + lax.broadcasted_iota(jnp.int32, (sub, QT), 0)
        rank = jnp.zeros((sub, QT), jnp.int32)
        for i in range(n_sel):
            other = score_t[i:i + 1]
            if i < v * sub:
                beats = other >= mine
            elif i >= (v + 1) * sub:
                beats = other > mine
            else:
                beats = (other > mine) | ((other == mine) & (i < jj))
            rank = rank + beats.astype(jnp.int32)
        ranks.append(rank)
    rank_t = jnp.concatenate(ranks + [jnp.full((LANES - n_sel, QT), n_sel, jnp.int32)], axis=0)
    chosen = (rank_t < k_sel).astype(F32).T
    mt_ref[0, 0] = jnp.where(chosen[:, :SEL_BLOCK] > 0.5, 0.0, MASK_TERM).astype(BF16)


def _cmp_branch(q, kc, vc, t_cmp, overlap):
    b, s, _ = q.shape
    nc = kc.shape[2]
    n_sel = s // SEL_BLOCK
    return pl.pallas_call(
        functools.partial(_cmp_kernel, n_sel=n_sel, k_sel=min(N_SELECT, n_sel)),
        out_shape=(jax.ShapeDtypeStruct((b, s, N_HEADS * HEAD_DIM), F32),
                   jax.ShapeDtypeStruct((b, N_GROUPS, s, SEL_BLOCK), BF16)),
        grid=(s // QT, N_GROUPS, b),
        in_specs=[pl.BlockSpec((1, QT, GROUP_DIM), lambda qi, gi, bi: (bi, qi, gi)),
                  pl.BlockSpec((1, 1, nc, HEAD_DIM), lambda qi, gi, bi: (bi, gi, 0, 0)),
                  pl.BlockSpec((1, 1, nc, HEAD_DIM), lambda qi, gi, bi: (bi, gi, 0, 0)),
                  pl.BlockSpec((HEADS_PER_GROUP, QT, nc), lambda qi, gi, bi: (gi, qi, 0)),
                  _resident(overlap.shape)],
        out_specs=(pl.BlockSpec((1, QT, GROUP_DIM), lambda qi, gi, bi: (bi, qi, gi)),
                   pl.BlockSpec((1, 1, QT, SEL_BLOCK), lambda qi, gi, bi: (bi, gi, qi, 0))),
        compiler_params=_params(3),
        name="cmp_branch",
    )(q, kc, vc, t_cmp, overlap)


def _selwin_kernel(q_ref, mt_ref, ksa_ref, vs_ref, kw_ref, vw_ref, tsel_ref, twin_ref, oc_ref, gt_ref,
                   o_ref, s_scr, m_scr, l_scr, acc_scr):
    qi = pl.program_id(2)
    rows = HEADS_PER_GROUP * QT
    q = q_ref[0]
    mt = mt_ref[0, 0]
    heads = [q[:, r * HEAD_DIM:(r + 1) * HEAD_DIM] for r in range(HEADS_PER_GROUP)]
    q_aug = jnp.concatenate([jnp.concatenate([hd, mt], axis=1) for hd in heads], axis=0)
    q_plain = jnp.concatenate(heads, axis=0)
    blocks_per_tile = KT // LANES
    n_tiles = (qi * QT + QT + KT - 1) // KT

    m_scr[...] = jnp.full((rows, LANES), -1e20, F32)

    def scores(kt, carry):
        k0 = pl.multiple_of(kt * KT, KT)
        s = _dot_nt(q_aug, ksa_ref[0, 0, pl.ds(k0, KT), :])
        for mb in range(blocks_per_tile):
            idx = jnp.clip(qi - (kt * blocks_per_tile + mb), -1, N_FAR) + 1
            for r in range(HEADS_PER_GROUP):
                rs = slice(r * QT, (r + 1) * QT)
                cs = slice(mb * LANES, (mb + 1) * LANES)
                sb = s[rs, cs] + tsel_ref[r, idx]
                s_scr[kt, rs, cs] = sb
                m_scr[rs, :] = jnp.maximum(m_scr[rs, :], sb)
        return carry

    lax.fori_loop(0, n_tiles, scores, 0)
    m_b = jnp.broadcast_to(jnp.max(m_scr[...], axis=-1, keepdims=True), (rows, LANES))

    l_scr[...] = jnp.zeros((rows, LANES), F32)
    acc_scr[...] = jnp.zeros((rows, HEAD_DIM), F32)

    def weighted(kt, carry):
        k0 = pl.multiple_of(kt * KT, KT)
        ps = []
        for mb in range(blocks_per_tile):
            p = jnp.exp(s_scr[kt, :, mb * LANES:(mb + 1) * LANES] - m_b)
            l_scr[...] += p
            ps.append(p.astype(BF16))
        acc_scr[...] += _dot(jnp.concatenate(ps, axis=1), vs_ref[0, 0, pl.ds(k0, KT), :])
        return carry

    lax.fori_loop(0, n_tiles, weighted, 0)
    l_sel = jnp.sum(l_scr[...], axis=-1, keepdims=True)
    o_sel = acc_scr[...] / jnp.maximum(l_sel, 1e-30)

    first_blk = jnp.maximum(qi - (N_WIN_BLOCKS - 1), 0)
    w0 = pl.multiple_of(first_blk * LANES, LANES)
    width = N_WIN_BLOCKS * LANES
    s_w = _dot_nt(q_plain, kw_ref[0, 0, pl.ds(w0, width), :])
    parts = []
    for r in range(HEADS_PER_GROUP):
        row = []
        for mb in range(N_WIN_BLOCKS):
            a = qi - (first_blk + mb)
            idx = jnp.where(a < 0, 0, a + 1)
            row.append(s_w[r * QT:(r + 1) * QT, mb * LANES:(mb + 1) * LANES] + twin_ref[r, idx])
        parts.append(jnp.concatenate(row, axis=1))
    s_w = jnp.concatenate(parts, axis=0)
    m_w = jnp.max(s_w, axis=-1, keepdims=True)
    e_w = jnp.exp(s_w - m_w)
    l_w = jnp.sum(e_w, axis=-1, keepdims=True)
    o_win = _dot(e_w.astype(BF16), vw_ref[0, 0, pl.ds(w0, width), :]) / jnp.maximum(l_w, 1e-30)

    gates = gt_ref[0]
    o_cmp = oc_ref[0]
    outs = []
    for r in range(HEADS_PER_GROUP):
        rs = slice(r * QT, (r + 1) * QT)
        g_c, g_s, g_w = (gates[:, 3 * r + j:3 * r + j + 1] for j in range(3))
        outs.append(g_c * o_cmp[:, r * HEAD_DIM:(r + 1) * HEAD_DIM] + g_s * o_sel[rs] + g_w * o_win[rs])
    o_ref[0] = jnp.concatenate(outs, axis=1).astype(BF16)


def _selwin_branch(q, mterm, ksa, vs, kw, vw, t_sel, t_win, o_cmp, gates):
    b, s, _ = q.shape
    rows = HEADS_PER_GROUP * QT
    kv = lambda width: pl.BlockSpec((1, 1, s, width), lambda gi, bi, qi: (bi, gi, 0, 0))
    tab = lambda n: pl.BlockSpec((HEADS_PER_GROUP, n, LANES, LANES), lambda gi, bi, qi: (gi, 0, 0, 0))
    qtile = pl.BlockSpec((1, QT, GROUP_DIM), lambda gi, bi, qi: (bi, qi, gi))
    return pl.pallas_call(
        _selwin_kernel,
        out_shape=jax.ShapeDtypeStruct((b, s, N_HEADS * HEAD_DIM), BF16),
        grid=(N_GROUPS, b, s // QT),
        in_specs=[qtile,
                  pl.BlockSpec((1, 1, QT, SEL_BLOCK), lambda gi, bi, qi: (bi, gi, qi, 0)),
                  kv(2 * HEAD_DIM), kv(HEAD_DIM), kv(HEAD_DIM), kv(HEAD_DIM),
                  tab(t_sel.shape[1]), tab(t_win.shape[1]),
                  qtile,
                  pl.BlockSpec((1, QT, LANES), lambda gi, bi, qi: (bi, qi, gi))],
        out_specs=qtile,
        scratch_shapes=[pltpu.VMEM(((s + KT - 1) // KT, rows, KT), F32),
                        pltpu.VMEM((rows, LANES), F32),
                        pltpu.VMEM((rows, LANES), F32),
                        pltpu.VMEM((rows, HEAD_DIM), F32)],
        compiler_params=_params(3),
        name="selwin_branch",
    )(q, mterm, ksa, vs, kw, vw, t_sel, t_win, o_cmp, gates)


def _proj_out_kernel(x_ref, o_ref_in, w_ref, out_ref):
    out_ref[...] = x_ref[...] + _dot(o_ref_in[...], w_ref[...])


def _proj_out(x2, o2, w_out, *, tm=512):
    t, d = x2.shape
    tile = pl.BlockSpec((tm, d), lambda i: (i, 0))
    return pl.pallas_call(
        _proj_out_kernel,
        out_shape=jax.ShapeDtypeStruct((t, d), F32),
        grid=(t // tm,),
        in_specs=[tile, pl.BlockSpec((tm, o2.shape[1]), lambda i: (i, 0)), _resident(w_out.shape)],
        out_specs=tile,
        compiler_params=_params(1),
        name="proj_out",
    )(x2, o2, w_out)


def _nsa_layer(x3, norm_g, w_in, pos_k, pos_v, wk1, wk2, wv1, wv2, w_out, tables):
    b, s, d = x3.shape
    assert s % KT == 0 and s >= N_WIN_BLOCKS * LANES and s // SEL_BLOCK <= SEL_BLOCK
    t_sel, t_win, t_cmp = tables
    q_dim = N_HEADS * HEAD_DIM
    wq = w_in[:, :q_dim].astype(BF16)
    wkv = w_in[:, q_dim:q_dim + 6 * KV_DIM].astype(BF16)
    wg = w_in[:, q_dim + 6 * KV_DIM:].reshape(d, N_GROUPS, 3 * HEADS_PER_GROUP)
    wg = jnp.pad(wg, ((0, 0), (0, 0), (0, LANES - 3 * HEADS_PER_GROUP))).reshape(d, N_GROUPS * LANES).astype(BF16)
    q, kc_raw, vc_raw, ksa, vs, kw, vw, gates = _nsa_in(x3, norm_g, wq, wkv, wg)

    nc = s // CMP_STRIDE
    chunk = CMP_STRIDE * HEAD_DIM
    half = lambda w: w.reshape(2, chunk, -1).astype(BF16)
    kc, vc = _compress(kc_raw.reshape(b, N_GROUPS, nc, chunk), vc_raw.reshape(b, N_GROUPS, nc, chunk),
                       pos_k.reshape(2, chunk), pos_v.reshape(2, chunk),
                       half(wk1), wk2.astype(BF16), half(wv1), wv2.astype(BF16))

    n_cmp = (s - CMP_LEN) // CMP_STRIDE + 1
    c_start = jnp.arange(nc)[:, None] * CMP_STRIDE
    n_start = jnp.arange(LANES)[None, :] * SEL_BLOCK
    overlap = ((c_start < n_start + SEL_BLOCK) & (c_start + CMP_LEN > n_start)
               & (jnp.arange(nc)[:, None] < n_cmp) & (jnp.arange(LANES)[None, :] < s // SEL_BLOCK))
    o_cmp, mterm = _cmp_branch(q, kc, vc, t_cmp, overlap.astype(BF16))
    o = _selwin_branch(q, mterm, ksa, vs, kw, vw, t_sel, t_win, o_cmp, gates)
    return _proj_out(x3.reshape(b * s, d), o.reshape(b * s, q_dim), w_out.astype(BF16)).reshape(b, s, d)


def _conv_layer(x3, norm_g, w_pw1, b_pw1, w_dw, b_dw, ln_g, ln_b, w_pw2, b_pw2):
    b, s, d = x3.shape
    row = lambda v: v.reshape(1, -1)
    u = _conv_in(x3.reshape(b * s, d), norm_g, w_pw1.astype(BF16), row(b_pw1))
    return _conv_out(u.reshape(b, s, d), x3, w_dw, row(b_dw), row(ln_g), row(ln_b), w_pw2.astype(BF16), row(b_pw2))


def kernel(x, p, rel_bias, ffn1_norm, ffn1_w_gate, ffn1_w_up, ffn1_w_down, mix_norm, ffn2_norm, ffn2_w_gate, ffn2_w_up, ffn2_w_down, ple_norm, ple_w_gate, ple_w_in, conv_w_pw1, conv_b_pw1, conv_w_dw, conv_b_dw, conv_ln_g, conv_ln_b, conv_w_pw2, conv_b_pw2, nsa_w_in, nsa_cmp_pos_k, nsa_cmp_pos_v, nsa_cmp_wk1, nsa_cmp_wk2, nsa_cmp_wv1, nsa_cmp_wv2, nsa_w_out, final_norm):
    b, s, d = x.shape
    depth = ffn1_norm.shape[0]
    row = lambda v: v.reshape(1, -1)
    tables = _bias_tables(rel_bias, s)
    x2 = x.reshape(b * s, d)
    for i in range(depth):
        x2 = _ffn(x2, row(ffn1_norm[i]), ffn1_w_gate[i].astype(BF16), ffn1_w_up[i].astype(BF16),
                  ffn1_w_down[i].astype(BF16))
        j = i // 2
        x3 = x2.reshape(b, s, d)
        if i % 2 == 0:
            x3 = _conv_layer(x3, row(mix_norm[i]), conv_w_pw1[j], conv_b_pw1[j], conv_w_dw[j], conv_b_dw[j],
                             conv_ln_g[j], conv_ln_b[j], conv_w_pw2[j], conv_b_pw2[j])
        else:
            x3 = _nsa_layer(x3, row(mix_norm[i]), nsa_w_in[j], nsa_cmp_pos_k[j], nsa_cmp_pos_v[j],
                            nsa_cmp_wk1[j], nsa_cmp_wk2[j], nsa_cmp_wv1[j], nsa_cmp_wv2[j], nsa_w_out[j], tables)
        x2 = _ffn(x3.reshape(b * s, d), row(ffn2_norm[i]), ffn2_w_gate[i].astype(BF16),
                  ffn2_w_up[i].astype(BF16), ffn2_w_down[i].astype(BF16))
        x2 = _ple(x2, p[i].reshape(b * s, -1), row(ple_norm[i]), ple_w_gate[i].astype(BF16),
                  ple_w_in[i].astype(BF16), row(final_norm), final=(i == depth - 1))
    return x2.reshape(b, s, d)
```

```python
import functools
import math

import jax
import jax.numpy as jnp
from jax import lax
from jax.experimental import pallas as pl
from jax.experimental.pallas import tpu as pltpu

F32 = jnp.float32
BF16 = jnp.bfloat16

D_MODEL = 1024
D_FF = 2816
CONV_KERNEL = 31
HEAD_DIM = 64
N_HEADS = 16
N_GROUPS = 4
HEADS_PER_GROUP = 4
GROUP_DIM = HEADS_PER_GROUP * HEAD_DIM
KV_DIM = N_GROUPS * HEAD_DIM
CMP_LEN = 32
CMP_STRIDE = 16
SEL_BLOCK = 64
N_SELECT = 16
WINDOW = 512
N_BUCKETS = 32
MAX_EXACT = 16
MAX_DISTANCE = 2048
RMS_EPS = 1e-6
LN_EPS = 1e-5
FORCE_SCORE = 1e9

LANES = 128
SUBLANES = 8
QT = 128
KT = 256
CQ = 256
GATE_ROWS = 16
HALO = 32
MASKED = -1e30
MASK_TERM = -(2.0 ** 100)
M_INIT = -1e20
N_FAR = MAX_DISTANCE // LANES + 1
N_WIN_BLOCKS = WINDOW // LANES + 1
VMEM_LIMIT = 56 * 1024 * 1024


def _params(n_axes):
    return pltpu.CompilerParams(dimension_semantics=("arbitrary",) * n_axes,
                                vmem_limit_bytes=VMEM_LIMIT)


def _resident(shape):
    nd = len(shape)
    return pl.BlockSpec(shape, lambda *_: (0,) * nd, pipeline_mode=pl.Buffered(1))


def _rms(x, g):
    return x * lax.rsqrt(jnp.mean(x * x, axis=-1, keepdims=True) + RMS_EPS) * g


def _dot(a, b):
    return jnp.dot(a, b, preferred_element_type=F32)


def _ffn_kernel(x_ref, g_ref, wg_ref, wu_ref, wd_ref, o_ref, *, ff_chunk):
    x = x_ref[...]
    xn = _rms(x, g_ref[...]).astype(BF16)
    acc = jnp.zeros_like(x)
    for c in range(D_FF // ff_chunk):
        sl = slice(c * ff_chunk, (c + 1) * ff_chunk)
        gate = _dot(xn, wg_ref[:, sl])
        up = _dot(xn, wu_ref[:, sl])
        h = (gate * jax.nn.sigmoid(gate)) * up
        acc = acc + _dot(h.astype(BF16), wd_ref[sl, :])
    o_ref[...] = x + 0.5 * acc


def _ffn(x2, norm_g, wg, wu, wd, *, tm=512, ff_chunk=1408):
    t, d = x2.shape
    tile = pl.BlockSpec((tm, d), lambda i: (i, 0))
    return pl.pallas_call(
        functools.partial(_ffn_kernel, ff_chunk=ff_chunk),
        out_shape=jax.ShapeDtypeStruct((t, d), F32),
        grid=(t // tm,),
        in_specs=[tile, _resident((1, d)), _resident((d, D_FF)), _resident((d, D_FF)), _resident((D_FF, d))],
        out_specs=tile,
        compiler_params=_params(1),
        name="ffn",
    )(x2, norm_g, wg, wu, wd)


def _ple_kernel(x_ref, p_ref, g_ref, wgate_ref, win_ref, fin_ref, o_ref, *, final):
    x = x_ref[...]
    gate = jax.nn.sigmoid(_dot(_rms(x, g_ref[...]).astype(BF16), wgate_ref[...]))
    y = x + gate * _dot(p_ref[...].astype(BF16), win_ref[...])
    if final:
        y = _rms(y, fin_ref[...])
    o_ref[...] = y


def _ple(x2, p2, norm_g, w_gate, w_in, final_g, *, final, tm=512):
    t, d = x2.shape
    pd = p2.shape[1]
    tile = pl.BlockSpec((tm, d), lambda i: (i, 0))
    return pl.pallas_call(
        functools.partial(_ple_kernel, final=final),
        out_shape=jax.ShapeDtypeStruct((t, d), F32),
        grid=(t // tm,),
        in_specs=[tile, pl.BlockSpec((tm, pd), lambda i: (i, 0)), _resident((1, d)),
                  _resident((d, d)), _resident((pd, d)), _resident((1, d))],
        out_specs=tile,
        compiler_params=_params(1),
        name="ple",
    )(x2, p2, norm_g, w_gate, w_in, final_g)


def _conv_in_kernel(x_ref, g_ref, w_ref, b_ref, o_ref):
    hn = _rms(x_ref[...], g_ref[...]).astype(BF16)
    d = o_ref.shape[-1]
    a = _dot(hn, w_ref[:, :d]) + b_ref[:, :d]
    gt = _dot(hn, w_ref[:, d:]) + b_ref[:, d:]
    o_ref[...] = a * jax.nn.sigmoid(gt)


def _conv_in(x2, norm_g, w_pw1, b_pw1, *, tm=512):
    t, d = x2.shape
    tile = pl.BlockSpec((tm, d), lambda i: (i, 0))
    return pl.pallas_call(
        _conv_in_kernel,
        out_shape=jax.ShapeDtypeStruct((t, d), F32),
        grid=(t // tm,),
        in_specs=[tile, _resident((1, d)), _resident((d, 2 * d)), _resident((1, 2 * d))],
        out_specs=tile,
        compiler_params=_params(1),
        name="conv_in",
    )(x2, norm_g, w_pw1, b_pw1)


def _conv_out_kernel(u_ref, halo_ref, x_ref, wdw_ref, bdw_ref, lng_ref, lnb_ref, w2_ref, b2_ref,
                     o_ref, pad_ref, conv_ref, *, ts):
    first = pl.program_id(1) == 0
    pad_ref[0:HALO, :] = jnp.where(first, 0.0, halo_ref[0])
    pad_ref[HALO:, :] = u_ref[0]
    off = HALO - (CONV_KERNEL - 1)
    d = u_ref.shape[-1]
    for c in range(d // LANES):
        cs = slice(c * LANES, (c + 1) * LANES)
        acc = jnp.zeros((ts, LANES), F32)
        for k in range(CONV_KERNEL):
            acc = acc + wdw_ref[k:k + 1, cs] * pad_ref[off + k:off + k + ts, cs]
        conv_ref[:, cs] = acc + bdw_ref[:, cs]
    u = conv_ref[...]
    mu = jnp.mean(u, axis=-1, keepdims=True)
    var = jnp.mean(jnp.square(u - mu), axis=-1, keepdims=True)
    y = (u - mu) * lax.rsqrt(var + LN_EPS) * lng_ref[...] + lnb_ref[...]
    y = y * jax.nn.sigmoid(y)
    o_ref[0] = x_ref[0] + _dot(y.astype(BF16), w2_ref[...]) + b2_ref[...]


def _conv_out(u3, x3, w_dw, b_dw, ln_g, ln_b, w_pw2, b_pw2, *, ts=256):
    b, s, d = u3.shape
    tile = pl.BlockSpec((1, ts, d), lambda bi, i: (bi, i, 0))
    halo = pl.BlockSpec((1, HALO, d), lambda bi, i: (bi, jnp.maximum(i * (ts // HALO) - 1, 0), 0))
    return pl.pallas_call(
        functools.partial(_conv_out_kernel, ts=ts),
        out_shape=jax.ShapeDtypeStruct((b, s, d), F32),
        grid=(b, s // ts),
        in_specs=[tile, halo, tile, _resident((CONV_KERNEL, d)), _resident((1, d)), _resident((1, d)),
                  _resident((1, d)), _resident((d, d)), _resident((1, d))],
        out_specs=tile,
        scratch_shapes=[pltpu.VMEM((ts + HALO, d), F32), pltpu.VMEM((ts, d), F32)],
        compiler_params=_params(2),
        name="conv_out",
    )(u3, u3, x3, w_dw, b_dw, ln_g, ln_b, w_pw2, b_pw2)


def _nsa_in_kernel(x_ref, g_ref, wq_ref, wkv_ref, wgt_ref,
                   qt_ref, kc_ref, vc_ref, ksa_ref, kwa_ref, vst_ref, vwt_ref, gt_ref, *, tm):
    hn = _rms(x_ref[0], g_ref[...]).astype(BF16)
    q = _dot(hn, wq_ref[...]) * (HEAD_DIM ** -0.5)
    heads_per_chunk = LANES // HEAD_DIM

    def transposed_heads(cols, chunk):
        t = cols[:, chunk * LANES:(chunk + 1) * LANES].T
        return [t[hh * HEAD_DIM:(hh + 1) * HEAD_DIM] for hh in range(heads_per_chunk)]

    for chunk in range(N_HEADS // heads_per_chunk):
        for hh, qt in enumerate(transposed_heads(q, chunk)):
            g, r = divmod(chunk * heads_per_chunk + hh, HEADS_PER_GROUP)
            qt_ref[0, g, r] = qt.astype(BF16)

    gates = jax.nn.sigmoid(_dot(hn, wgt_ref[...]))
    for g in range(N_GROUPS):
        gt_ref[0, g] = gates[:, g * LANES:(g + 1) * LANES].T[:GATE_ROWS]

    kv = _dot(hn, wkv_ref[...])
    pos = pl.program_id(1) * tm + lax.broadcasted_iota(jnp.int32, (tm, SEL_BLOCK), 0)
    blk = lax.broadcasted_iota(jnp.int32, (tm, SEL_BLOCK), 1)
    onehot = jnp.where(pos // SEL_BLOCK == blk, 1.0, 0.0).astype(BF16)
    nothing = jnp.zeros((tm, SEL_BLOCK), BF16)
    for g in range(N_GROUPS):
        def part(j):
            lo = j * KV_DIM + g * HEAD_DIM
            return kv[:, lo:lo + HEAD_DIM]
        kc_ref[0, g] = part(0)
        vc_ref[0, g] = part(1)
        ksa_ref[0, g] = jnp.concatenate([part(2).astype(BF16), onehot], axis=1)
        kwa_ref[0, g] = jnp.concatenate([part(4).astype(BF16), nothing], axis=1)
    v_sel = kv[:, 3 * KV_DIM:4 * KV_DIM]
    v_win = kv[:, 5 * KV_DIM:6 * KV_DIM]
    for chunk in range(N_GROUPS // heads_per_chunk):
        for hh, (vs, vw) in enumerate(zip(transposed_heads(v_sel, chunk), transposed_heads(v_win, chunk))):
            g = chunk * heads_per_chunk + hh
            for j in range(tm // KT):
                vst_ref[0, g, j] = vs[:, j * KT:(j + 1) * KT].astype(BF16)
            for j in range(tm // LANES):
                vwt_ref[0, g, j] = vw[:, j * LANES:(j + 1) * LANES].astype(BF16)


def _nsa_in(x3, norm_g, wq, wkv, wgt, *, tm=512):
    b, s, d = x3.shape
    per_group = lambda width: pl.BlockSpec((1, N_GROUPS, tm, width), lambda bi, i: (bi, 0, i, 0))
    kv_shape = lambda width, dt: jax.ShapeDtypeStruct((b, N_GROUPS, s, width), dt)
    return pl.pallas_call(
        functools.partial(_nsa_in_kernel, tm=tm),
        out_shape=(jax.ShapeDtypeStruct((b, N_GROUPS, HEADS_PER_GROUP, HEAD_DIM, s), BF16),
                   kv_shape(HEAD_DIM, F32), kv_shape(HEAD_DIM, F32),
                   kv_shape(2 * HEAD_DIM, BF16), kv_shape(2 * HEAD_DIM, BF16),
                   jax.ShapeDtypeStruct((b, N_GROUPS, s // KT, HEAD_DIM, KT), BF16),
                   jax.ShapeDtypeStruct((b, N_GROUPS, s // LANES, HEAD_DIM, LANES), BF16),
                   jax.ShapeDtypeStruct((b, N_GROUPS, GATE_ROWS, s), F32)),
        grid=(b, s // tm),
        in_specs=[pl.BlockSpec((1, tm, d), lambda bi, i: (bi, i, 0)), _resident((1, d)),
                  _resident(wq.shape), _resident(wkv.shape), _resident(wgt.shape)],
        out_specs=(pl.BlockSpec((1, N_GROUPS, HEADS_PER_GROUP, HEAD_DIM, tm), lambda bi, i: (bi, 0, 0, 0, i)),
                   per_group(HEAD_DIM), per_group(HEAD_DIM), per_group(2 * HEAD_DIM), per_group(2 * HEAD_DIM),
                   pl.BlockSpec((1, N_GROUPS, tm // KT, HEAD_DIM, KT), lambda bi, i: (bi, 0, i, 0, 0)),
                   pl.BlockSpec((1, N_GROUPS, tm // LANES, HEAD_DIM, LANES), lambda bi, i: (bi, 0, i, 0, 0)),
                   pl.BlockSpec((1, N_GROUPS, GATE_ROWS, tm), lambda bi, i: (bi, 0, 0, i))),
        compiler_params=_params(2),
        name="nsa_in",
    )(x3, norm_g, wq, wkv, wgt)


def _compress_kernel(kx_ref, vx_ref, pk_ref, pv_ref, wk1_ref, wk2_ref, wv1_ref, wv2_ref, kc_ref, vct_ref):
    def mlp(x_ref, p_ref, w1_ref, w2_ref):
        x = x_ref[0, 0]
        nc = x.shape[0]
        first = _dot((x + p_ref[0:1, :]).astype(BF16), w1_ref[0])
        second = _dot((x + p_ref[1:2, :]).astype(BF16), w1_ref[1])
        hidden = first + pltpu.roll(second, nc - 1, 0)
        return _dot(jax.nn.gelu(hidden).astype(BF16), w2_ref[...])
    kc_ref[0, 0] = mlp(kx_ref, pk_ref, wk1_ref, wk2_ref)[:, :HEAD_DIM].astype(BF16)
    vct_ref[0, 0] = mlp(vx_ref, pv_ref, wv1_ref, wv2_ref).T[:HEAD_DIM].astype(BF16)


def _compress(kx, vx, pos_k, pos_v, wk1, wk2, wv1, wv2):
    b, g, nc, width = kx.shape
    blk = pl.BlockSpec((1, 1, nc, width), lambda bi, gi: (bi, gi, 0, 0))
    return pl.pallas_call(
        _compress_kernel,
        out_shape=(jax.ShapeDtypeStruct((b, g, nc, HEAD_DIM), BF16),
                   jax.ShapeDtypeStruct((b, g, HEAD_DIM, nc), BF16)),
        grid=(b, g),
        in_specs=[blk, blk, _resident(pos_k.shape), _resident(pos_v.shape), _resident(wk1.shape),
                  _resident(wk2.shape), _resident(wv1.shape), _resident(wv2.shape)],
        out_specs=(pl.BlockSpec((1, 1, nc, HEAD_DIM), lambda bi, gi: (bi, gi, 0, 0)),
                   pl.BlockSpec((1, 1, HEAD_DIM, nc), lambda bi, gi: (bi, gi, 0, 0))),
        compiler_params=_params(2),
        name="compress",
    )(kx, vx, pos_k, pos_v, wk1, wk2, wv1, wv2)


def _t5_bucket(dist):
    n = jnp.maximum(dist, 0)
    nf = jnp.maximum(n, 1).astype(F32)
    large = MAX_EXACT + (jnp.log(nf / MAX_EXACT) / math.log(MAX_DISTANCE / MAX_EXACT)
                         * (N_BUCKETS - MAX_EXACT)).astype(jnp.int32)
    large = jnp.minimum(large, N_BUCKETS - 1)
    return jnp.where(n < MAX_EXACT, n, large)


def _bias_table_kernel(rb_ref, bucket_ref, o_ref):
    h = pl.program_id(0)
    bucket = bucket_ref[...]
    out = jnp.full(bucket.shape, MASKED, F32)
    for bkt in range(N_BUCKETS):
        out = jnp.where(bucket == bkt, rb_ref[bkt, h], out)
    o_ref[0] = out


def _bias_table(rel_bias, bucket, *, tr):
    rows, lanes = bucket.shape
    return pl.pallas_call(
        _bias_table_kernel,
        out_shape=jax.ShapeDtypeStruct((N_HEADS, rows, lanes), F32),
        grid=(N_HEADS, rows // tr),
        in_specs=[pl.BlockSpec(memory_space=pltpu.SMEM), pl.BlockSpec((tr, lanes), lambda h, i: (i, 0))],
        out_specs=pl.BlockSpec((1, tr, lanes), lambda h, i: (h, i, 0)),
        compiler_params=_params(2),
        name="bias_table",
    )(rel_bias, bucket)


def _bias_tables(rel_bias, s):
    nc = s // CMP_STRIDE
    key = jnp.arange(LANES)[:, None]
    qry = jnp.arange(LANES)[None, :]

    def toeplitz_blocks(offsets, max_dist):
        dist = LANES * offsets[:, None, None] + qry[None] - key[None]
        valid = (dist >= 0) & (dist < max_dist)
        return jnp.where(valid, _t5_bucket(dist), -1).astype(jnp.int32).reshape(-1, LANES)

    sel_bucket = toeplitz_blocks(jnp.arange(-1, N_FAR + 1), 1 << 30)
    win_bucket = toeplitz_blocks(jnp.arange(-1, N_WIN_BLOCKS), WINDOW)
    cmp_dist = jnp.arange(s)[None, :] - (jnp.arange(nc)[:, None] * CMP_STRIDE + CMP_LEN - 1)
    cmp_bucket = jnp.where(cmp_dist >= 0, _t5_bucket(cmp_dist), -1).astype(jnp.int32)
    t_sel = _bias_table(rel_bias, sel_bucket, tr=LANES).reshape(N_HEADS, N_FAR + 2, LANES, LANES)
    t_win = _bias_table(rel_bias, win_bucket, tr=LANES).reshape(N_HEADS, N_WIN_BLOCKS + 1, LANES, LANES)
    t_cmp = _bias_table(rel_bias, cmp_bucket, tr=nc)
    return t_sel, t_win, t_cmp


def _sublane_fold(x, op):
    parts = [x[k * SUBLANES:(k + 1) * SUBLANES] for k in range(x.shape[0] // SUBLANES)]
    while len(parts) > 1:
        parts = [op(parts[k], parts[k + 1]) for k in range(0, len(parts) - 1, 2)] + parts[len(parts) & ~1:]
    return parts[0]


def _cmp_kernel(qt_ref, kc_ref, vct_ref, tc_ref, ovt_ref, oct_ref, mtt_ref, *, n_sel, k_sel):
    heads = range(HEADS_PER_GROUP)
    q_t = jnp.concatenate([qt_ref[0, 0, r] for r in heads], axis=1)
    bias = jnp.concatenate([tc_ref[r] for r in heads], axis=1)
    s = _dot(kc_ref[0, 0], q_t) + bias
    m = jnp.max(s, axis=0, keepdims=True)
    e = jnp.where(bias > 0.5 * MASKED, jnp.exp(s - m), 0.0)
    pr = e * (1.0 / jnp.maximum(jnp.sum(e, axis=0, keepdims=True), 1e-30))
    o_t = _dot(vct_ref[0, 0], pr.astype(BF16))
    for r in heads:
        oct_ref[0, 0, r] = o_t[:, r * CQ:(r + 1) * CQ]

    psum = pr[:, 0:CQ]
    for r in heads[1:]:
        psum = psum + pr[:, r * CQ:(r + 1) * CQ]
    hi = psum.astype(BF16)
    rem = psum - hi.astype(F32)
    mid = rem.astype(BF16)
    lo = (rem - mid.astype(F32)).astype(BF16)
    ov_t = ovt_ref[...]
    imp = _dot(ov_t, hi) + _dot(ov_t, mid) + _dot(ov_t, lo)

    t = pl.program_id(0) * CQ + lax.broadcasted_iota(jnp.int32, (LANES, CQ), 1)
    j = lax.broadcasted_iota(jnp.int32, (LANES, CQ), 0)
    cur = t // SEL_BLOCK
    forced = (j == 0) | (j == cur) | (j == cur - 1)
    score = jnp.where(forced, FORCE_SCORE, jnp.where(j * SEL_BLOCK <= t, imp, -1.0))

    terms = []
    for v in range(n_sel // SUBLANES):
        mine = score[v * SUBLANES:(v + 1) * SUBLANES]
        jj = v * SUBLANES + lax.broadcasted_iota(jnp.int32, (SUBLANES, CQ), 0)
        rank = jnp.zeros((SUBLANES, CQ), jnp.int32)
        for i in range(n_sel):
            other = score[i:i + 1]
            if i < v * SUBLANES:
                beats = other >= mine
            elif i >= (v + 1) * SUBLANES:
                beats = other > mine
            else:
                beats = (other > mine) | ((other == mine) & (i < jj))
            rank = rank + beats.astype(jnp.int32)
        terms.append(jnp.where(rank < k_sel, 0.0, MASK_TERM))
    if n_sel < SEL_BLOCK:
        terms.append(jnp.zeros((SEL_BLOCK - n_sel, CQ), F32))
    mtt_ref[0, 0] = jnp.concatenate(terms, axis=0).astype(BF16)


def _cmp_branch(q_t, kc, vc_t, t_cmp, overlap_t):
    b, _, _, _, s = q_t.shape
    nc = kc.shape[2]
    n_sel = s // SEL_BLOCK
    qspec = pl.BlockSpec((1, 1, HEADS_PER_GROUP, HEAD_DIM, CQ), lambda qi, gi, bi: (bi, gi, 0, 0, qi))
    return pl.pallas_call(
        functools.partial(_cmp_kernel, n_sel=n_sel, k_sel=min(N_SELECT, n_sel)),
        out_shape=(jax.ShapeDtypeStruct((b, N_GROUPS, HEADS_PER_GROUP, HEAD_DIM, s), F32),
                   jax.ShapeDtypeStruct((b, N_GROUPS, SEL_BLOCK, s), BF16)),
        grid=(s // CQ, N_GROUPS, b),
        in_specs=[qspec,
                  pl.BlockSpec((1, 1, nc, HEAD_DIM), lambda qi, gi, bi: (bi, gi, 0, 0)),
                  pl.BlockSpec((1, 1, HEAD_DIM, nc), lambda qi, gi, bi: (bi, gi, 0, 0)),
                  pl.BlockSpec((HEADS_PER_GROUP, nc, CQ), lambda qi, gi, bi: (gi, 0, qi)),
                  _resident(overlap_t.shape)],
        out_specs=(qspec, pl.BlockSpec((1, 1, SEL_BLOCK, CQ), lambda qi, gi, bi: (bi, gi, 0, qi))),
        compiler_params=_params(3),
        name="cmp_branch",
    )(q_t, kc, vc_t, t_cmp, overlap_t)


def _selwin_kernel(qta_ref, qtb_ref, mta_ref, mtb_ref, ksa_ref, kwa_ref, vst_ref, vwt_ref, tsel_ref, twin_ref,
                   oca_ref, ocb_ref, gta_ref, gtb_ref, oa_ref, ob_ref, qa_scr, s_scr, acc_scr, *, n_q):
    i = pl.program_id(2)
    heads = range(HEADS_PER_GROUP)
    cols = HEADS_PER_GROUP * QT
    n_slots = s_scr.shape[0]
    blocks_per_tile = KT // LANES
    q_tiles = (i, n_q - 1 - i)
    n_first = i // blocks_per_tile + 1

    for w, (qt_ref, mt_ref) in enumerate(((qta_ref, mta_ref), (qtb_ref, mtb_ref))):
        qa_scr[w] = jnp.concatenate(
            [jnp.concatenate([qt_ref[0, 0, r], mt_ref[0, 0]], axis=0) for r in heads], axis=1)

    def slot_info(slot):
        second = slot >= n_first
        w = second.astype(jnp.int32)
        kt = jnp.where(second, slot - n_first, slot)
        tq = jnp.where(second, q_tiles[1], q_tiles[0])
        return second, w, kt, tq

    m_run = [jnp.full((SUBLANES, cols), M_INIT, F32)] * 2
    for slot in range(n_slots):
        second, w, kt, tq = slot_info(slot)
        k0 = pl.multiple_of(kt * KT, KT)
        s = _dot(ksa_ref[0, 0, pl.ds(k0, KT), :], qa_scr[w])
        folded = []
        for r in heads:
            cs = slice(r * QT, (r + 1) * QT)
            fold = None
            for mb in range(blocks_per_tile):
                rs = slice(mb * LANES, (mb + 1) * LANES)
                idx = jnp.clip(tq - (kt * blocks_per_tile + mb), -1, N_FAR) + 1
                sb = s[rs, cs] + tsel_ref[r, idx]
                s_scr[slot, rs, cs] = sb
                part = _sublane_fold(sb, jnp.maximum)
                fold = part if fold is None else jnp.maximum(fold, part)
            folded.append(fold)
        tile_max = jnp.concatenate(folded, axis=1)
        m_run = [jnp.maximum(m_run[0], jnp.where(second, M_INIT, tile_max)),
                 jnp.maximum(m_run[1], jnp.where(second, tile_max, M_INIT))]
    m_fin = [jnp.max(m, axis=0, keepdims=True) for m in m_run]

    acc_scr[...] = jnp.zeros(acc_scr.shape, F32)
    l_run = [jnp.zeros((SUBLANES, cols), F32)] * 2
    for slot in range(n_slots):
        second, w, kt, tq = slot_info(slot)
        p = jnp.exp(s_scr[slot] - jnp.where(second, m_fin[1], m_fin[0]))
        tile_sum = _sublane_fold(p, jnp.add)
        l_run = [l_run[0] + jnp.where(second, 0.0, tile_sum), l_run[1] + jnp.where(second, tile_sum, 0.0)]
        acc_scr[w] += _dot(vst_ref[0, 0, kt], p.astype(BF16))

    for w, (oc_ref, gt_ref, o_ref) in enumerate(((oca_ref, gta_ref, oa_ref), (ocb_ref, gtb_ref, ob_ref))):
        tq = q_tiles[w]
        l_sel = jnp.sum(l_run[w], axis=0, keepdims=True)
        o_sel = acc_scr[w] * (1.0 / jnp.maximum(l_sel, 1e-30))

        first_blk = jnp.maximum(tq - (N_WIN_BLOCKS - 1), 0)
        w0 = pl.multiple_of(first_blk * LANES, LANES)
        s_w = _dot(kwa_ref[0, 0, pl.ds(w0, N_WIN_BLOCKS * LANES), :], qa_scr[w])
        rows = []
        for mb in range(N_WIN_BLOCKS):
            a = tq - (first_blk + mb)
            idx = jnp.where(a < 0, 0, a + 1)
            rows.append(jnp.concatenate(
                [s_w[mb * LANES:(mb + 1) * LANES, r * QT:(r + 1) * QT] + twin_ref[r, idx] for r in heads], axis=1))
        s_w = jnp.concatenate(rows, axis=0)
        e_w = jnp.exp(s_w - jnp.max(s_w, axis=0, keepdims=True))
        l_w = jnp.sum(e_w, axis=0, keepdims=True)
        o_win = None
        for mb in range(N_WIN_BLOCKS):
            term = _dot(vwt_ref[0, 0, first_blk + mb], e_w[mb * LANES:(mb + 1) * LANES].astype(BF16))
            o_win = term if o_win is None else o_win + term
        o_win = o_win * (1.0 / jnp.maximum(l_w, 1e-30))

        gates = gt_ref[0, 0]
        mixed = []
        for r in heads:
            cs = slice(r * QT, (r + 1) * QT)
            g_c, g_s, g_w = (gates[3 * r + j:3 * r + j + 1] for j in range(3))
            mixed.append(g_c * oc_ref[0, 0, r] + g_s * o_sel[:, cs] + g_w * o_win[:, cs])
        pairs = [jnp.concatenate(mixed[k:k + 2], axis=0).T for k in range(0, HEADS_PER_GROUP, 2)]
        o_ref[0] = jnp.concatenate(pairs, axis=1).astype(BF16)


def _selwin_branch(q_t, mterm_t, ksa, kwa, vs_t, vw_t, t_sel, t_win, o_cmp_t, gates_t):
    b, _, _, _, s = q_t.shape
    n_q = s // QT
    half = n_q // 2
    n_slots = half + 1
    cols = HEADS_PER_GROUP * QT
    fwd = lambda gi, bi, i: i
    bwd = lambda gi, bi, i: n_q - 1 - i
    qspec = lambda pick: pl.BlockSpec((1, 1, HEADS_PER_GROUP, HEAD_DIM, QT),
                                      lambda gi, bi, i: (bi, gi, 0, 0, pick(gi, bi, i)))
    rowspec = lambda rows, pick: pl.BlockSpec((1, 1, rows, QT), lambda gi, bi, i: (bi, gi, 0, pick(gi, bi, i)))
    whole = lambda arr: pl.BlockSpec((1, 1) + arr.shape[2:], lambda gi, bi, i: (bi, gi) + (0,) * (arr.ndim - 2))
    tab = lambda n: pl.BlockSpec((HEADS_PER_GROUP, n, LANES, LANES), lambda gi, bi, i: (gi, 0, 0, 0))
    out_shape = jax.ShapeDtypeStruct((b, s // 2, N_HEADS * HEAD_DIM), BF16)
    return pl.pallas_call(
        functools.partial(_selwin_kernel, n_q=n_q),
        out_shape=(out_shape, out_shape),
        grid=(N_GROUPS, b, half),
        in_specs=[qspec(fwd), qspec(bwd), rowspec(SEL_BLOCK, fwd), rowspec(SEL_BLOCK, bwd),
                  whole(ksa), whole(kwa), whole(vs_t), whole(vw_t),
                  tab(t_sel.shape[1]), tab(t_win.shape[1]),
                  qspec(fwd), qspec(bwd), rowspec(GATE_ROWS, fwd), rowspec(GATE_ROWS, bwd)],
        out_specs=(pl.BlockSpec((1, QT, GROUP_DIM), lambda gi, bi, i: (bi, i, gi)),
                   pl.BlockSpec((1, QT, GROUP_DIM), lambda gi, bi, i: (bi, half - 1 - i, gi))),
        scratch_shapes=[pltpu.VMEM((2, 2 * HEAD_DIM, cols), BF16),
                        pltpu.VMEM((n_slots, KT, cols), F32),
                        pltpu.VMEM((2, HEAD_DIM, cols), F32)],
        compiler_params=_params(3),
        name="selwin_branch",
    )(q_t, q_t, mterm_t, mterm_t, ksa, kwa, vs_t, vw_t, t_sel, t_win, o_cmp_t, o_cmp_t, gates_t, gates_t)


def _proj_out_kernel(x_ref, o_ref_in, w_ref, out_ref):
    out_ref[...] = x_ref[...] + _dot(o_ref_in[...], w_ref[...])


def _proj_out(x2, o2, w_out, *, tm=512):
    t, d = x2.shape
    tile = pl.BlockSpec((tm, d), lambda i: (i, 0))
    return pl.pallas_call(
        _proj_out_kernel,
        out_shape=jax.ShapeDtypeStruct((t, d), F32),
        grid=(t // tm,),
        in_specs=[tile, pl.BlockSpec((tm, o2.shape[1]), lambda i: (i, 0)), _resident(w_out.shape)],
        out_specs=tile,
        compiler_params=_params(1),
        name="proj_out",
    )(x2, o2, w_out)


def _nsa_layer(x3, norm_g, w_in, pos_k, pos_v, wk1, wk2, wv1, wv2, w_out, tables):
    b, s, d = x3.shape
    n_q = s // QT
    assert KT == 2 * QT and n_q % 2 == 0 and s % CQ == 0 and n_q >= N_WIN_BLOCKS and s // SEL_BLOCK <= SEL_BLOCK
    t_sel, t_win, t_cmp = tables
    q_dim = N_HEADS * HEAD_DIM
    wq = w_in[:, :q_dim].astype(BF16)
    wkv = w_in[:, q_dim:q_dim + 6 * KV_DIM].astype(BF16)
    wg = w_in[:, q_dim + 6 * KV_DIM:].reshape(d, N_GROUPS, 3 * HEADS_PER_GROUP)
    wg = jnp.pad(wg, ((0, 0), (0, 0), (0, LANES - 3 * HEADS_PER_GROUP))).reshape(d, N_GROUPS * LANES).astype(BF16)
    q_t, kc_raw, vc_raw, ksa, kwa, vs_t, vw_t, gates_t = _nsa_in(x3, norm_g, wq, wkv, wg)

    nc = s // CMP_STRIDE
    chunk = CMP_STRIDE * HEAD_DIM
    half = lambda w: w.reshape(2, chunk, -1).astype(BF16)
    widen = lambda w: jnp.pad(w, ((0, 0), (0, LANES - HEAD_DIM))).astype(BF16)
    kc, vc_t = _compress(kc_raw.reshape(b, N_GROUPS, nc, chunk), vc_raw.reshape(b, N_GROUPS, nc, chunk),
                         pos_k.reshape(2, chunk), pos_v.reshape(2, chunk),
                         half(wk1), widen(wk2), half(wv1), widen(wv2))

    n_cmp = (s - CMP_LEN) // CMP_STRIDE + 1
    c_start = jnp.arange(nc)[None, :] * CMP_STRIDE
    n_start = jnp.arange(LANES)[:, None] * SEL_BLOCK
    overlap_t = ((c_start < n_start + SEL_BLOCK) & (c_start + CMP_LEN > n_start)
                 & (jnp.arange(nc)[None, :] < n_cmp) & (jnp.arange(LANES)[:, None] < s // SEL_BLOCK))
    o_cmp_t, mterm_t = _cmp_branch(q_t, kc, vc_t, t_cmp, overlap_t.astype(BF16))
    o_lo, o_hi = _selwin_branch(q_t, mterm_t, ksa, kwa, vs_t, vw_t, t_sel, t_win, o_cmp_t, gates_t)
    o = jnp.concatenate([o_lo, o_hi], axis=1)
    return _proj_out(x3.reshape(b * s, d), o.reshape(b * s, q_dim), w_out.astype(BF16)).reshape(b, s, d)


def _conv_layer(x3, norm_g, w_pw1, b_pw1, w_dw, b_dw, ln_g, ln_b, w_pw2, b_pw2):
    b, s, d = x3.shape
    row = lambda v: v.reshape(1, -1)
    u = _conv_in(x3.reshape(b * s, d), norm_g, w_pw1.astype(BF16), row(b_pw1))
    return _conv_out(u.reshape(b, s, d), x3, w_dw, row(b_dw), row(ln_g), row(ln_b), w_pw2.astype(BF16), row(b_pw2))


def kernel(x, p, rel_bias, ffn1_norm, ffn1_w_gate, ffn1_w_up, ffn1_w_down, mix_norm, ffn2_norm, ffn2_w_gate, ffn2_w_up, ffn2_w_down, ple_norm, ple_w_gate, ple_w_in, conv_w_pw1, conv_b_pw1, conv_w_dw, conv_b_dw, conv_ln_g, conv_ln_b, conv_w_pw2, conv_b_pw2, nsa_w_in, nsa_cmp_pos_k, nsa_cmp_pos_v, nsa_cmp_wk1, nsa_cmp_wk2, nsa_cmp_wv1, nsa_cmp_wv2, nsa_w_out, final_norm):
    b, s, d = x.shape
    depth = ffn1_norm.shape[0]
    row = lambda v: v.reshape(1, -1)
    tables = _bias_tables(rel_bias, s)
    x2 = x.reshape(b * s, d)
    for i in range(depth):
        x2 = _ffn(x2, row(ffn1_norm[i]), ffn1_w_gate[i].astype(BF16), ffn1_w_up[i].astype(BF16),
                  ffn1_w_down[i].astype(BF16))
        j = i // 2
        x3 = x2.reshape(b, s, d)
        if i % 2 == 0:
            x3 = _conv_layer(x3, row(mix_norm[i]), conv_w_pw1[j], conv_b_pw1[j], conv_w_dw[j], conv_b_dw[j],
                             conv_ln_g[j], conv_ln_b[j], conv_w_pw2[j], conv_b_pw2[j])
        else:
            x3 = _nsa_layer(x3, row(mix_norm[i]), nsa_w_in[j], nsa_cmp_pos_k[j], nsa_cmp_pos_v[j],
                            nsa_cmp_wk1[j], nsa_cmp_wk2[j], nsa_cmp_wv1[j], nsa_cmp_wv2[j], nsa_w_out[j], tables)
        x2 = _ffn(x3.reshape(b * s, d), row(ffn2_norm[i]), ffn2_w_gate[i].astype(BF16),
                  ffn2_w_up[i].astype(BF16), ffn2_w_down[i].astype(BF16))
        x2 = _ple(x2, p[i].reshape(b * s, -1), row(ple_norm[i]), ple_w_gate[i].astype(BF16),
                  ple_w_in[i].astype(BF16), row(final_norm), final=(i == depth - 1))
    return x2.reshape(b, s, d)
```

```python
import functools
import math

import jax
import jax.numpy as jnp
from jax import lax
from jax.experimental import pallas as pl
from jax.experimental.pallas import tpu as pltpu

F32 = jnp.float32
BF16 = jnp.bfloat16

D_MODEL = 1024
D_FF = 2816
CONV_KERNEL = 31
HEAD_DIM = 64
N_HEADS = 16
N_GROUPS = 4
HEADS_PER_GROUP = 4
GROUP_DIM = HEADS_PER_GROUP * HEAD_DIM
KV_DIM = N_GROUPS * HEAD_DIM
CMP_LEN = 32
CMP_STRIDE = 16
SEL_BLOCK = 64
N_SELECT = 16
WINDOW = 512
N_BUCKETS = 32
MAX_EXACT = 16
MAX_DISTANCE = 2048
RMS_EPS = 1e-6
LN_EPS = 1e-5
FORCE_SCORE = 1e9

LANES = 128
SUBLANES = 8
QT = 128
KT = 256
CQ = 256
GATE_ROWS = 16
SCORE_LOOKAHEAD = 2
HALO = 32
MASKED = -1e30
MASK_TERM = -(2.0 ** 100)
M_INIT = -1e20
LOG2E = math.log2(math.e)
V_ROWS = HEAD_DIM + 16
PART_ROWS = HEAD_DIM + SUBLANES
N_FAR = MAX_DISTANCE // LANES + 1
N_WIN_BLOCKS = WINDOW // LANES + 1
VMEM_LIMIT = 56 * 1024 * 1024


def _params(n_axes):
    return pltpu.CompilerParams(dimension_semantics=("arbitrary",) * n_axes,
                                vmem_limit_bytes=VMEM_LIMIT)


def _resident(shape):
    nd = len(shape)
    return pl.BlockSpec(shape, lambda *_: (0,) * nd, pipeline_mode=pl.Buffered(1))


def _rms(x, g):
    return x * lax.rsqrt(jnp.mean(x * x, axis=-1, keepdims=True) + RMS_EPS) * g


def _dot(a, b):
    return jnp.dot(a, b, preferred_element_type=F32)


def _ffn_kernel(x_ref, g_ref, wg_ref, wu_ref, wd_ref, o_ref, *, ff_chunk):
    x = x_ref[...]
    xn = _rms(x, g_ref[...]).astype(BF16)
    acc = jnp.zeros_like(x)
    for c in range(D_FF // ff_chunk):
        sl = slice(c * ff_chunk, (c + 1) * ff_chunk)
        gate = _dot(xn, wg_ref[:, sl])
        up = _dot(xn, wu_ref[:, sl])
        h = (gate * jax.nn.sigmoid(gate)) * up
        acc = acc + _dot(h.astype(BF16), wd_ref[sl, :])
    o_ref[...] = x + 0.5 * acc


def _ffn(x2, norm_g, wg, wu, wd, *, tm=512, ff_chunk=1408):
    t, d = x2.shape
    tile = pl.BlockSpec((tm, d), lambda i: (i, 0))
    return pl.pallas_call(
        functools.partial(_ffn_kernel, ff_chunk=ff_chunk),
        out_shape=jax.ShapeDtypeStruct((t, d), F32),
        grid=(t // tm,),
        in_specs=[tile, _resident((1, d)), _resident((d, D_FF)), _resident((d, D_FF)), _resident((D_FF, d))],
        out_specs=tile,
        compiler_params=_params(1),
        name="ffn",
    )(x2, norm_g, wg, wu, wd)


def _ple_kernel(x_ref, p_ref, g_ref, wgate_ref, win_ref, fin_ref, o_ref, *, final):
    x = x_ref[...]
    gate = jax.nn.sigmoid(_dot(_rms(x, g_ref[...]).astype(BF16), wgate_ref[...]))
    y = x + gate * _dot(p_ref[...].astype(BF16), win_ref[...])
    if final:
        y = _rms(y, fin_ref[...])
    o_ref[...] = y


def _ple(x2, p2, norm_g, w_gate, w_in, final_g, *, final, tm=512):
    t, d = x2.shape
    pd = p2.shape[1]
    tile = pl.BlockSpec((tm, d), lambda i: (i, 0))
    return pl.pallas_call(
        functools.partial(_ple_kernel, final=final),
        out_shape=jax.ShapeDtypeStruct((t, d), F32),
        grid=(t // tm,),
        in_specs=[tile, pl.BlockSpec((tm, pd), lambda i: (i, 0)), _resident((1, d)),
                  _resident((d, d)), _resident((pd, d)), _resident((1, d))],
        out_specs=tile,
        compiler_params=_params(1),
        name="ple",
    )(x2, p2, norm_g, w_gate, w_in, final_g)


def _conv_in_kernel(x_ref, g_ref, w_ref, b_ref, o_ref):
    hn = _rms(x_ref[...], g_ref[...]).astype(BF16)
    d = o_ref.shape[-1]
    a = _dot(hn, w_ref[:, :d]) + b_ref[:, :d]
    gt = _dot(hn, w_ref[:, d:]) + b_ref[:, d:]
    o_ref[...] = a * jax.nn.sigmoid(gt)


def _conv_in(x2, norm_g, w_pw1, b_pw1, *, tm=512):
    t, d = x2.shape
    tile = pl.BlockSpec((tm, d), lambda i: (i, 0))
    return pl.pallas_call(
        _conv_in_kernel,
        out_shape=jax.ShapeDtypeStruct((t, d), F32),
        grid=(t // tm,),
        in_specs=[tile, _resident((1, d)), _resident((d, 2 * d)), _resident((1, 2 * d))],
        out_specs=tile,
        compiler_params=_params(1),
        name="conv_in",
    )(x2, norm_g, w_pw1, b_pw1)


def _conv_out_kernel(u_ref, halo_ref, x_ref, wdw_ref, bdw_ref, lng_ref, lnb_ref, w2_ref, b2_ref,
                     o_ref, pad_ref, conv_ref, *, ts):
    first = pl.program_id(1) == 0
    pad_ref[0, 0:HALO, :] = jnp.where(first, 0.0, halo_ref[0])
    pad_ref[0, HALO:, :] = u_ref[0]
    shifted_rows = ts + HALO - SUBLANES
    for ph in range(1, SUBLANES):
        pad_ref[ph, 0:shifted_rows, :] = pad_ref[0, ph:ph + shifted_rows, :]
    off = HALO - (CONV_KERNEL - 1)
    d = u_ref.shape[-1]
    for c in range(d // LANES):
        cs = slice(c * LANES, (c + 1) * LANES)
        acc = jnp.zeros((ts, LANES), F32)
        for k in range(CONV_KERNEL):
            ph = (off + k) % SUBLANES
            base = off + k - ph
            acc = acc + wdw_ref[k:k + 1, cs] * pad_ref[ph, base:base + ts, cs]
        conv_ref[:, cs] = acc + bdw_ref[:, cs]
    u = conv_ref[...]
    mu = jnp.mean(u, axis=-1, keepdims=True)
    var = jnp.mean(jnp.square(u - mu), axis=-1, keepdims=True)
    y = (u - mu) * lax.rsqrt(var + LN_EPS) * lng_ref[...] + lnb_ref[...]
    y = y * jax.nn.sigmoid(y)
    o_ref[0] = x_ref[0] + _dot(y.astype(BF16), w2_ref[...]) + b2_ref[...]


def _conv_out(u3, x3, w_dw, b_dw, ln_g, ln_b, w_pw2, b_pw2, *, ts=256):
    b, s, d = u3.shape
    tile = pl.BlockSpec((1, ts, d), lambda bi, i: (bi, i, 0))
    halo = pl.BlockSpec((1, HALO, d), lambda bi, i: (bi, jnp.maximum(i * (ts // HALO) - 1, 0), 0))
    return pl.pallas_call(
        functools.partial(_conv_out_kernel, ts=ts),
        out_shape=jax.ShapeDtypeStruct((b, s, d), F32),
        grid=(b, s // ts),
        in_specs=[tile, halo, tile, _resident((CONV_KERNEL, d)), _resident((1, d)), _resident((1, d)),
                  _resident((1, d)), _resident((d, d)), _resident((1, d))],
        out_specs=tile,
        scratch_shapes=[pltpu.VMEM((SUBLANES, ts + HALO, d), F32), pltpu.VMEM((ts, d), F32)],
        compiler_params=_params(2),
        name="conv_out",
    )(u3, u3, x3, w_dw, b_dw, ln_g, ln_b, w_pw2, b_pw2)


def _nsa_in_kernel(x_ref, g_ref, wq_ref, wkv_ref, wgt_ref,
                   qt_ref, kc_ref, vc_ref, ksa_ref, kwa_ref, vst_ref, vwt_ref, gt_ref, *, tm):
    hn = _rms(x_ref[0], g_ref[...]).astype(BF16)
    q = _dot(hn, wq_ref[...]) * (HEAD_DIM ** -0.5 * LOG2E)
    heads_per_chunk = LANES // HEAD_DIM

    def transposed_heads(cols, chunk):
        t = cols[:, chunk * LANES:(chunk + 1) * LANES].T
        return [t[hh * HEAD_DIM:(hh + 1) * HEAD_DIM] for hh in range(heads_per_chunk)]

    for chunk in range(N_HEADS // heads_per_chunk):
        for hh, qt in enumerate(transposed_heads(q, chunk)):
            g, r = divmod(chunk * heads_per_chunk + hh, HEADS_PER_GROUP)
            qt_ref[0, g, r] = qt.astype(BF16)

    gates = jax.nn.sigmoid(_dot(hn, wgt_ref[...]))
    for g in range(N_GROUPS):
        gt_ref[0, g] = gates[:, g * LANES:(g + 1) * LANES].T[:GATE_ROWS]

    kv = _dot(hn, wkv_ref[...])
    pos = pl.program_id(1) * tm + lax.broadcasted_iota(jnp.int32, (tm, SEL_BLOCK), 0)
    blk = lax.broadcasted_iota(jnp.int32, (tm, SEL_BLOCK), 1)
    onehot = jnp.where(pos // SEL_BLOCK == blk, 1.0, 0.0).astype(BF16)
    nothing = jnp.zeros((tm, SEL_BLOCK), BF16)
    for g in range(N_GROUPS):
        def part(j):
            lo = j * KV_DIM + g * HEAD_DIM
            return kv[:, lo:lo + HEAD_DIM]
        kc_ref[0, g] = part(0)
        vc_ref[0, g] = part(1)
        ksa_ref[0, g] = jnp.concatenate([part(2).astype(BF16), onehot], axis=1)
        kwa_ref[0, g] = jnp.concatenate([part(4).astype(BF16), nothing], axis=1)
    v_sel = kv[:, 3 * KV_DIM:4 * KV_DIM]
    v_win = kv[:, 5 * KV_DIM:6 * KV_DIM]
    for chunk in range(N_GROUPS // heads_per_chunk):
        for hh, (vs, vw) in enumerate(zip(transposed_heads(v_sel, chunk), transposed_heads(v_win, chunk))):
            g = chunk * heads_per_chunk + hh
            for j in range(tm // KT):
                vst_ref[0, g, j, 0:HEAD_DIM, :] = vs[:, j * KT:(j + 1) * KT].astype(BF16)
                vst_ref[0, g, j, HEAD_DIM:, :] = jnp.ones((V_ROWS - HEAD_DIM, KT), BF16)
            for j in range(tm // LANES):
                vwt_ref[0, g, j, 0:HEAD_DIM, :] = vw[:, j * LANES:(j + 1) * LANES].astype(BF16)
                vwt_ref[0, g, j, HEAD_DIM:, :] = jnp.ones((V_ROWS - HEAD_DIM, LANES), BF16)


def _nsa_in(x3, norm_g, wq, wkv, wgt, *, tm=512):
    b, s, d = x3.shape
    per_group = lambda width: pl.BlockSpec((1, N_GROUPS, tm, width), lambda bi, i: (bi, 0, i, 0))
    kv_shape = lambda width, dt: jax.ShapeDtypeStruct((b, N_GROUPS, s, width), dt)
    return pl.pallas_call(
        functools.partial(_nsa_in_kernel, tm=tm),
        out_shape=(jax.ShapeDtypeStruct((b, N_GROUPS, HEADS_PER_GROUP, HEAD_DIM, s), BF16),
                   kv_shape(HEAD_DIM, F32), kv_shape(HEAD_DIM, F32),
                   kv_shape(2 * HEAD_DIM, BF16), kv_shape(2 * HEAD_DIM, BF16),
                   jax.ShapeDtypeStruct((b, N_GROUPS, s // KT, V_ROWS, KT), BF16),
                   jax.ShapeDtypeStruct((b, N_GROUPS, s // LANES, V_ROWS, LANES), BF16),
                   jax.ShapeDtypeStruct((b, N_GROUPS, GATE_ROWS, s), F32)),
        grid=(b, s // tm),
        in_specs=[pl.BlockSpec((1, tm, d), lambda bi, i: (bi, i, 0)), _resident((1, d)),
                  _resident(wq.shape), _resident(wkv.shape), _resident(wgt.shape)],
        out_specs=(pl.BlockSpec((1, N_GROUPS, HEADS_PER_GROUP, HEAD_DIM, tm), lambda bi, i: (bi, 0, 0, 0, i)),
                   per_group(HEAD_DIM), per_group(HEAD_DIM), per_group(2 * HEAD_DIM), per_group(2 * HEAD_DIM),
                   pl.BlockSpec((1, N_GROUPS, tm // KT, V_ROWS, KT), lambda bi, i: (bi, 0, i, 0, 0)),
                   pl.BlockSpec((1, N_GROUPS, tm // LANES, V_ROWS, LANES), lambda bi, i: (bi, 0, i, 0, 0)),
                   pl.BlockSpec((1, N_GROUPS, GATE_ROWS, tm), lambda bi, i: (bi, 0, 0, i))),
        compiler_params=_params(2),
        name="nsa_in",
    )(x3, norm_g, wq, wkv, wgt)


def _compress_kernel(kx_ref, vx_ref, pk_ref, pv_ref, wk1_ref, wk2_ref, wv1_ref, wv2_ref, kc_ref, vct_ref, *, nc):
    def mlp(x_ref, p_ref, w1_ref, w2_ref):
        first = second = None
        for l in range(CMP_STRIDE):
            rows = x_ref[0, 0, pl.ds(l, nc, stride=CMP_STRIDE), :]
            lo = _dot((rows + p_ref[l:l + 1, :]).astype(BF16), w1_ref[l])
            hi = _dot((rows + p_ref[CMP_STRIDE + l:CMP_STRIDE + l + 1, :]).astype(BF16), w1_ref[CMP_STRIDE + l])
            first = lo if first is None else first + lo
            second = hi if second is None else second + hi
        hidden = first + pltpu.roll(second, nc - 1, 0)
        return _dot(jax.nn.gelu(hidden).astype(BF16), w2_ref[...])
    kc_ref[0, 0] = mlp(kx_ref, pk_ref, wk1_ref, wk2_ref)[:, :HEAD_DIM].astype(BF16)
    vct_ref[0, 0] = mlp(vx_ref, pv_ref, wv1_ref, wv2_ref).T[:HEAD_DIM].astype(BF16)


def _compress(kx, vx, pos_k, pos_v, wk1, wk2, wv1, wv2):
    b, g, s, width = kx.shape
    nc = s // CMP_STRIDE
    blk = pl.BlockSpec((1, 1, s, width), lambda bi, gi: (bi, gi, 0, 0))
    return pl.pallas_call(
        functools.partial(_compress_kernel, nc=nc),
        out_shape=(jax.ShapeDtypeStruct((b, g, nc, HEAD_DIM), BF16),
                   jax.ShapeDtypeStruct((b, g, HEAD_DIM, nc), BF16)),
        grid=(b, g),
        in_specs=[blk, blk, _resident(pos_k.shape), _resident(pos_v.shape), _resident(wk1.shape),
                  _resident(wk2.shape), _resident(wv1.shape), _resident(wv2.shape)],
        out_specs=(pl.BlockSpec((1, 1, nc, HEAD_DIM), lambda bi, gi: (bi, gi, 0, 0)),
                   pl.BlockSpec((1, 1, HEAD_DIM, nc), lambda bi, gi: (bi, gi, 0, 0))),
        compiler_params=_params(2),
        name="compress",
    )(kx, vx, pos_k, pos_v, wk1, wk2, wv1, wv2)


def _t5_bucket(dist):
    n = jnp.maximum(dist, 0)
    nf = jnp.maximum(n, 1).astype(F32)
    large = MAX_EXACT + (jnp.log(nf / MAX_EXACT) / math.log(MAX_DISTANCE / MAX_EXACT)
                         * (N_BUCKETS - MAX_EXACT)).astype(jnp.int32)
    large = jnp.minimum(large, N_BUCKETS - 1)
    return jnp.where(n < MAX_EXACT, n, large)


def _bias_table_kernel(rb_ref, bucket_ref, o_ref):
    h = pl.program_id(0)
    bucket = bucket_ref[...]
    out = jnp.full(bucket.shape, MASKED, F32)
    for bkt in range(N_BUCKETS):
        out = jnp.where(bucket == bkt, rb_ref[bkt, h] * LOG2E, out)
    o_ref[0] = out


def _bias_table(rel_bias, bucket, *, tr):
    rows, lanes = bucket.shape
    return pl.pallas_call(
        _bias_table_kernel,
        out_shape=jax.ShapeDtypeStruct((N_HEADS, rows, lanes), F32),
        grid=(N_HEADS, rows // tr),
        in_specs=[pl.BlockSpec(memory_space=pltpu.SMEM), pl.BlockSpec((tr, lanes), lambda h, i: (i, 0))],
        out_specs=pl.BlockSpec((1, tr, lanes), lambda h, i: (h, i, 0)),
        compiler_params=_params(2),
        name="bias_table",
    )(rel_bias, bucket)


def _bias_tables(rel_bias, s):
    nc = s // CMP_STRIDE
    key = jnp.arange(LANES)[:, None]
    qry = jnp.arange(LANES)[None, :]

    def toeplitz_blocks(offsets, max_dist):
        dist = LANES * offsets[:, None, None] + qry[None] - key[None]
        valid = (dist >= 0) & (dist < max_dist)
        return jnp.where(valid, _t5_bucket(dist), -1).astype(jnp.int32).reshape(-1, LANES)

    sel_bucket = toeplitz_blocks(jnp.arange(-1, N_FAR + 1), 1 << 30)
    win_bucket = toeplitz_blocks(jnp.arange(-1, N_WIN_BLOCKS), WINDOW)
    cmp_dist = jnp.arange(s)[None, :] - (jnp.arange(nc)[:, None] * CMP_STRIDE + CMP_LEN - 1)
    cmp_bucket = jnp.where(cmp_dist >= 0, _t5_bucket(cmp_dist), -1).astype(jnp.int32)
    t_sel = _bias_table(rel_bias, sel_bucket, tr=sel_bucket.shape[0]).reshape(N_HEADS, N_FAR + 2, LANES, LANES)
    t_win = _bias_table(rel_bias, win_bucket, tr=win_bucket.shape[0]).reshape(N_HEADS, N_WIN_BLOCKS + 1, LANES, LANES)
    t_cmp = _bias_table(rel_bias, cmp_bucket, tr=nc)
    return t_sel, t_win, t_cmp


def _sublane_fold(x, op):
    parts = [x[k * SUBLANES:(k + 1) * SUBLANES] for k in range(x.shape[0] // SUBLANES)]
    while len(parts) > 1:
        parts = [op(parts[k], parts[k + 1]) for k in range(0, len(parts) - 1, 2)] + parts[len(parts) & ~1:]
    return parts[0]


def _cmp_kernel(qt_ref, kc_ref, vct_ref, tc_ref, ovt_ref, oct_ref, mtt_ref, *, n_sel, k_sel):
    heads = range(HEADS_PER_GROUP)
    q_t = jnp.concatenate([qt_ref[0, 0, r] for r in heads], axis=1)
    bias = jnp.concatenate([tc_ref[r] for r in heads], axis=1)
    s = _dot(kc_ref[0, 0], q_t) + bias
    m = jnp.max(s, axis=0, keepdims=True)
    e = jnp.where(bias > 0.5 * MASKED, jnp.exp2(s - m), 0.0)
    pr = e * (1.0 / jnp.maximum(jnp.sum(e, axis=0, keepdims=True), 1e-30))
    o_t = _dot(vct_ref[0, 0], pr.astype(BF16))
    for r in heads:
        oct_ref[0, 0, r] = o_t[:, r * CQ:(r + 1) * CQ]

    psum = pr[:, 0:CQ]
    for r in heads[1:]:
        psum = psum + pr[:, r * CQ:(r + 1) * CQ]
    hi = psum.astype(BF16)
    rem = psum - hi.astype(F32)
    mid = rem.astype(BF16)
    lo = (rem - mid.astype(F32)).astype(BF16)
    ov_t = ovt_ref[...]
    imp = _dot(ov_t, hi) + _dot(ov_t, mid) + _dot(ov_t, lo)

    t = pl.program_id(0) * CQ + lax.broadcasted_iota(jnp.int32, (LANES, CQ), 1)
    j = lax.broadcasted_iota(jnp.int32, (LANES, CQ), 0)
    cur = t // SEL_BLOCK
    forced = (j == 0) | (j == cur) | (j == cur - 1)
    score = jnp.where(forced, FORCE_SCORE, jnp.where(j * SEL_BLOCK <= t, imp, -1.0))

    terms = []
    for v in range(n_sel // SUBLANES):
        mine = score[v * SUBLANES:(v + 1) * SUBLANES]
        jj = v * SUBLANES + lax.broadcasted_iota(jnp.int32, (SUBLANES, CQ), 0)
        rank = jnp.zeros((SUBLANES, CQ), jnp.int32)
        for i in range(n_sel):
            other = score[i:i + 1]
            if i < v * SUBLANES:
                beats = other >= mine
            elif i >= (v + 1) * SUBLANES:
                beats = other > mine
            else:
                beats = (other > mine) | ((other == mine) & (i < jj))
            rank = rank + beats.astype(jnp.int32)
        terms.append(jnp.where(rank < k_sel, 0.0, MASK_TERM))
    if n_sel < SEL_BLOCK:
        terms.append(jnp.zeros((SEL_BLOCK - n_sel, CQ), F32))
    mtt_ref[0, 0] = jnp.concatenate(terms, axis=0).astype(BF16)


def _cmp_branch(q_t, kc, vc_t, t_cmp, overlap_t):
    b, _, _, _, s = q_t.shape
    nc = kc.shape[2]
    n_sel = s // SEL_BLOCK
    qspec = pl.BlockSpec((1, 1, HEADS_PER_GROUP, HEAD_DIM, CQ), lambda qi, gi, bi: (bi, gi, 0, 0, qi))
    return pl.pallas_call(
        functools.partial(_cmp_kernel, n_sel=n_sel, k_sel=min(N_SELECT, n_sel)),
        out_shape=(jax.ShapeDtypeStruct((b, N_GROUPS, HEADS_PER_GROUP, HEAD_DIM, s), F32),
                   jax.ShapeDtypeStruct((b, N_GROUPS, SEL_BLOCK, s), BF16)),
        grid=(s // CQ, N_GROUPS, b),
        in_specs=[qspec,
                  pl.BlockSpec((1, 1, nc, HEAD_DIM), lambda qi, gi, bi: (bi, gi, 0, 0)),
                  pl.BlockSpec((1, 1, HEAD_DIM, nc), lambda qi, gi, bi: (bi, gi, 0, 0)),
                  pl.BlockSpec((HEADS_PER_GROUP, nc, CQ), lambda qi, gi, bi: (gi, 0, qi)),
                  _resident(overlap_t.shape)],
        out_specs=(qspec, pl.BlockSpec((1, 1, SEL_BLOCK, CQ), lambda qi, gi, bi: (bi, gi, 0, qi))),
        compiler_params=_params(3),
        name="cmp_branch",
    )(q_t, kc, vc_t, t_cmp, overlap_t)


def _selwin_kernel(qta_ref, qtb_ref, mta_ref, mtb_ref, ksa_ref, kwa_ref, vst_ref, vwt_ref, tsel_ref, twin_ref,
                   oca_ref, ocb_ref, gta_ref, gtb_ref, oa_ref, ob_ref, qa_scr, part_scr, win_scr, *, n_q):
    i = pl.program_id(2)
    heads = range(HEADS_PER_GROUP)
    cols = HEADS_PER_GROUP * QT
    n_slots = part_scr.shape[0]
    blocks_per_tile = KT // LANES
    q_tiles = (i, n_q - 1 - i)
    n_first = i // blocks_per_tile + 1

    for w, (qt_ref, mt_ref) in enumerate(((qta_ref, mta_ref), (qtb_ref, mtb_ref))):
        qa_scr[w] = jnp.concatenate(
            [jnp.concatenate([qt_ref[0, 0, r], mt_ref[0, 0]], axis=0) for r in heads], axis=1)

    def slot_info(slot):
        second = slot >= n_first
        w = second.astype(jnp.int32)
        kt = jnp.where(second, slot - n_first, slot)
        tq = jnp.where(second, q_tiles[1], q_tiles[0])
        return second, w, kt, tq

    def normalised(aug):
        return aug[:HEAD_DIM] * (1.0 / jnp.maximum(aug[HEAD_DIM:HEAD_DIM + 1], 1e-30))

    def win_first_block(w):
        return jnp.maximum(q_tiles[w] - (N_WIN_BLOCKS - 1), 0)

    def win_scores(w):
        w0 = pl.multiple_of(win_first_block(w) * LANES, LANES)
        return _dot(kwa_ref[0, 0, pl.ds(w0, N_WIN_BLOCKS * LANES), :], qa_scr[w])

    def win_finish(w, s_w):
        first_blk = win_first_block(w)
        rows = []
        for mb in range(N_WIN_BLOCKS):
            a = q_tiles[w] - (first_blk + mb)
            idx = jnp.where(a < 0, 0, a + 1)
            rows.append(jnp.concatenate(
                [s_w[mb * LANES:(mb + 1) * LANES, r * QT:(r + 1) * QT] + twin_ref[r, idx] for r in heads], axis=1))
        s_w = jnp.concatenate(rows, axis=0)
        e_w = jnp.exp2(s_w - jnp.max(s_w, axis=0, keepdims=True)).astype(BF16)
        win = None
        for mb in range(N_WIN_BLOCKS):
            term = _dot(vwt_ref[0, 0, first_blk + mb], e_w[mb * LANES:(mb + 1) * LANES])
            win = term if win is None else win + term
        win_scr[w] = normalised(win)

    def sel_scores(slot):
        _, w, kt, _ = slot_info(slot)
        k0 = pl.multiple_of(kt * KT, KT)
        return _dot(ksa_ref[0, 0, pl.ds(k0, KT), :], qa_scr[w])

    m_slots = {}

    def sel_finish(slot, s):
        second, w, kt, tq = slot_info(slot)
        biased, folded = [], []
        for r in heads:
            cs = slice(r * QT, (r + 1) * QT)
            column, fold = [], None
            for mb in range(blocks_per_tile):
                idx = jnp.clip(tq - (kt * blocks_per_tile + mb), -1, N_FAR) + 1
                sb = s[mb * LANES:(mb + 1) * LANES, cs] + tsel_ref[r, idx]
                column.append(sb)
                part = _sublane_fold(sb, jnp.maximum)
                fold = part if fold is None else jnp.maximum(fold, part)
            biased.append(column)
            folded.append(fold)
        m_slot = jnp.maximum(jnp.max(jnp.concatenate(folded, axis=1), axis=0, keepdims=True), M_INIT)
        p = jnp.concatenate(
            [jnp.concatenate([jnp.exp2(biased[r][mb] - m_slot[:, r * QT:(r + 1) * QT]).astype(BF16) for r in heads],
                             axis=1) for mb in range(blocks_per_tile)], axis=0)
        part_scr[slot] = _dot(vst_ref[0, 0, kt], p)[:PART_ROWS]
        m_slots[slot] = m_slot

    items = [(win_scores, win_finish, w) for w in range(2)] + [(sel_scores, sel_finish, k) for k in range(n_slots)]
    pending = {}
    for n in range(len(items) + SCORE_LOOKAHEAD):
        if n < len(items):
            scores, _, arg = items[n]
            pending[n] = scores(arg)
        if n >= SCORE_LOOKAHEAD:
            _, finish, arg = items[n - SCORE_LOOKAHEAD]
            finish(arg, pending.pop(n - SCORE_LOOKAHEAD))

    max_first = (n_q // 2 - 1) // blocks_per_tile + 1
    seconds = [slot_info(slot)[0] for slot in range(n_slots)]
    m_fin = [functools.reduce(jnp.maximum, [jnp.where(seconds[k], M_INIT, m_slots[k]) for k in range(max_first)]),
             functools.reduce(jnp.maximum, [jnp.where(seconds[k], m_slots[k], M_INIT) if k < max_first
                                            else m_slots[k] for k in range(1, n_slots)])]
    o_sels = []
    for w in range(2):
        acc = None
        for slot in (range(max_first) if w == 0 else range(1, n_slots)):
            weight = jnp.exp2(m_slots[slot] - m_fin[w])
            if 0 < slot < max_first:
                owned = seconds[slot] if w == 1 else jnp.logical_not(seconds[slot])
                weight = jnp.where(owned, weight, 0.0)
            term = part_scr[slot] * weight
            acc = term if acc is None else acc + term
        o_sels.append(normalised(acc))

    for w, (oc_ref, gt_ref, o_ref) in enumerate(((oca_ref, gta_ref, oa_ref), (ocb_ref, gtb_ref, ob_ref))):
        o_sel = o_sels[w]
        o_win = win_scr[w]

        gates = gt_ref[0, 0]
        mixed = []
        for r in heads:
            cs = slice(r * QT, (r + 1) * QT)
            g_c, g_s, g_w = (gates[3 * r + j:3 * r + j + 1] for j in range(3))
            mixed.append(g_c * oc_ref[0, 0, r] + g_s * o_sel[:, cs] + g_w * o_win[:, cs])
        pairs = [jnp.concatenate(mixed[k:k + 2], axis=0).T for k in range(0, HEADS_PER_GROUP, 2)]
        o_ref[0] = jnp.concatenate(pairs, axis=1).astype(BF16)


def _selwin_branch(q_t, mterm_t, ksa, kwa, vs_t, vw_t, t_sel, t_win, o_cmp_t, gates_t):
    b, _, _, _, s = q_t.shape
    n_q = s // QT
    half = n_q // 2
    n_slots = half + 1
    cols = HEADS_PER_GROUP * QT
    fwd = lambda gi, bi, i: i
    bwd = lambda gi, bi, i: n_q - 1 - i
    qspec = lambda pick: pl.BlockSpec((1, 1, HEADS_PER_GROUP, HEAD_DIM, QT),
                                      lambda gi, bi, i: (bi, gi, 0, 0, pick(gi, bi, i)))
    rowspec = lambda rows, pick: pl.BlockSpec((1, 1, rows, QT), lambda gi, bi, i: (bi, gi, 0, pick(gi, bi, i)))
    whole = lambda arr: pl.BlockSpec((1, 1) + arr.shape[2:], lambda gi, bi, i: (bi, gi) + (0,) * (arr.ndim - 2))
    tab = lambda n: pl.BlockSpec((HEADS_PER_GROUP, n, LANES, LANES), lambda gi, bi, i: (gi, 0, 0, 0))
    out_shape = jax.ShapeDtypeStruct((b, s // 2, N_HEADS * HEAD_DIM), BF16)
    return pl.pallas_call(
        functools.partial(_selwin_kernel, n_q=n_q),
        out_shape=(out_shape, out_shape),
        grid=(N_GROUPS, b, half),
        in_specs=[qspec(fwd), qspec(bwd), rowspec(SEL_BLOCK, fwd), rowspec(SEL_BLOCK, bwd),
                  whole(ksa), whole(kwa), whole(vs_t), whole(vw_t),
                  tab(t_sel.shape[1]), tab(t_win.shape[1]),
                  qspec(fwd), qspec(bwd), rowspec(GATE_ROWS, fwd), rowspec(GATE_ROWS, bwd)],
        out_specs=(pl.BlockSpec((1, QT, GROUP_DIM), lambda gi, bi, i: (bi, i, gi)),
                   pl.BlockSpec((1, QT, GROUP_DIM), lambda gi, bi, i: (bi, half - 1 - i, gi))),
        scratch_shapes=[pltpu.VMEM((2, 2 * HEAD_DIM, cols), BF16),
                        pltpu.VMEM((n_slots, PART_ROWS, cols), F32),
                        pltpu.VMEM((2, HEAD_DIM, cols), F32)],
        compiler_params=_params(3),
        name="selwin_branch",
    )(q_t, q_t, mterm_t, mterm_t, ksa, kwa, vs_t, vw_t, t_sel, t_win, o_cmp_t, o_cmp_t, gates_t, gates_t)


def _proj_out_kernel(x_ref, lo_ref, hi_ref, w_ref, out_ref, *, half_tiles):
    o = jnp.where(pl.program_id(1) < half_tiles, lo_ref[0], hi_ref[0])
    out_ref[0] = x_ref[0] + _dot(o, w_ref[...])


def _proj_out(x3, o_lo, o_hi, w_out, *, tm=512):
    b, s, d = x3.shape
    half_tiles = s // 2 // tm
    tile = pl.BlockSpec((1, tm, d), lambda bi, i: (bi, i, 0))
    width = o_lo.shape[2]
    return pl.pallas_call(
        functools.partial(_proj_out_kernel, half_tiles=half_tiles),
        out_shape=jax.ShapeDtypeStruct((b, s, d), F32),
        grid=(b, s // tm),
        in_specs=[tile,
                  pl.BlockSpec((1, tm, width), lambda bi, i: (bi, jnp.minimum(i, half_tiles - 1), 0)),
                  pl.BlockSpec((1, tm, width), lambda bi, i: (bi, jnp.maximum(i - half_tiles, 0), 0)),
                  _resident(w_out.shape)],
        out_specs=tile,
        compiler_params=_params(2),
        name="proj_out",
    )(x3, o_lo, o_hi, w_out)


def _nsa_layer(x3, norm_g, w_in, pos_k, pos_v, wk1, wk2, wv1, wv2, w_out, tables):
    b, s, d = x3.shape
    n_q = s // QT
    assert KT == 2 * QT and n_q % 2 == 0 and s % CQ == 0 and n_q >= N_WIN_BLOCKS and s // SEL_BLOCK <= SEL_BLOCK
    t_sel, t_win, t_cmp = tables
    q_dim = N_HEADS * HEAD_DIM
    wq = w_in[:, :q_dim].astype(BF16)
    wkv = w_in[:, q_dim:q_dim + 6 * KV_DIM].astype(BF16)
    wg = w_in[:, q_dim + 6 * KV_DIM:].reshape(d, N_GROUPS, 3 * HEADS_PER_GROUP)
    wg = jnp.pad(wg, ((0, 0), (0, 0), (0, LANES - 3 * HEADS_PER_GROUP))).reshape(d, N_GROUPS * LANES).astype(BF16)
    q_t, kc_raw, vc_raw, ksa, kwa, vs_t, vw_t, gates_t = _nsa_in(x3, norm_g, wq, wkv, wg)

    nc = s // CMP_STRIDE
    per_row = lambda w: w.reshape(CMP_LEN, HEAD_DIM, -1).astype(BF16)
    widen = lambda w: jnp.pad(w, ((0, 0), (0, LANES - HEAD_DIM))).astype(BF16)
    kc, vc_t = _compress(kc_raw, vc_raw, pos_k, pos_v, per_row(wk1), widen(wk2), per_row(wv1), widen(wv2))

    n_cmp = (s - CMP_LEN) // CMP_STRIDE + 1
    c_start = jnp.arange(nc)[None, :] * CMP_STRIDE
    n_start = jnp.arange(LANES)[:, None] * SEL_BLOCK
    overlap_t = ((c_start < n_start + SEL_BLOCK) & (c_start + CMP_LEN > n_start)
                 & (jnp.arange(nc)[None, :] < n_cmp) & (jnp.arange(LANES)[:, None] < s // SEL_BLOCK))
    o_cmp_t, mterm_t = _cmp_branch(q_t, kc, vc_t, t_cmp, overlap_t.astype(BF16))
    o_lo, o_hi = _selwin_branch(q_t, mterm_t, ksa, kwa, vs_t, vw_t, t_sel, t_win, o_cmp_t, gates_t)
    return _proj_out(x3, o_lo, o_hi, w_out.astype(BF16))


def _conv_layer(x3, norm_g, w_pw1, b_pw1, w_dw, b_dw, ln_g, ln_b, w_pw2, b_pw2):
    b, s, d = x3.shape
    row = lambda v: v.reshape(1, -1)
    u = _conv_in(x3.reshape(b * s, d), norm_g, w_pw1.astype(BF16), row(b_pw1))
    return _conv_out(u.reshape(b, s, d), x3, w_dw, row(b_dw), row(ln_g), row(ln_b), w_pw2.astype(BF16), row(b_pw2))


def kernel(x, p, rel_bias, ffn1_norm, ffn1_w_gate, ffn1_w_up, ffn1_w_down, mix_norm, ffn2_norm, ffn2_w_gate, ffn2_w_up, ffn2_w_down, ple_norm, ple_w_gate, ple_w_in, conv_w_pw1, conv_b_pw1, conv_w_dw, conv_b_dw, conv_ln_g, conv_ln_b, conv_w_pw2, conv_b_pw2, nsa_w_in, nsa_cmp_pos_k, nsa_cmp_pos_v, nsa_cmp_wk1, nsa_cmp_wk2, nsa_cmp_wv1, nsa_cmp_wv2, nsa_w_out, final_norm):
    b, s, d = x.shape
    depth = ffn1_norm.shape[0]
    row = lambda v: v.reshape(1, -1)
    tables = _bias_tables(rel_bias, s)
    x2 = x.reshape(b * s, d)
    for i in range(depth):
        x2 = _ffn(x2, row(ffn1_norm[i]), ffn1_w_gate[i].astype(BF16), ffn1_w_up[i].astype(BF16),
                  ffn1_w_down[i].astype(BF16))
        j = i // 2
        x3 = x2.reshape(b, s, d)
        if i % 2 == 0:
            x3 = _conv_layer(x3, row(mix_norm[i]), conv_w_pw1[j], conv_b_pw1[j], conv_w_dw[j], conv_b_dw[j],
                             conv_ln_g[j], conv_ln_b[j], conv_w_pw2[j], conv_b_pw2[j])
        else:
            x3 = _nsa_layer(x3, row(mix_norm[i]), nsa_w_in[j], nsa_cmp_pos_k[j], nsa_cmp_pos_v[j],
                            nsa_cmp_wk1[j], nsa_cmp_wk2[j], nsa_cmp_wv1[j], nsa_cmp_wv2[j], nsa_w_out[j], tables)
        x2 = _ffn(x3.reshape(b * s, d), row(ffn2_norm[i]), ffn2_w_gate[i].astype(BF16),
                  ffn2_w_up[i].astype(BF16), ffn2_w_down[i].astype(BF16))
        x2 = _ple(x2, p[i].reshape(b * s, -1), row(ple_norm[i]), ple_w_gate[i].astype(BF16),
                  ple_w_in[i].astype(BF16), row(final_norm), final=(i == depth - 1))
    return x2.reshape(b, s, d)
```

```python
import functools
import math

import jax
import jax.numpy as jnp
from jax import lax
from jax.experimental import pallas as pl
from jax.experimental.pallas import tpu as pltpu

F32 = jnp.float32
BF16 = jnp.bfloat16

D_MODEL = 1024
D_FF = 2816
CONV_KERNEL = 31
HEAD_DIM = 64
N_HEADS = 16
N_GROUPS = 4
HEADS_PER_GROUP = 4
GROUP_DIM = HEADS_PER_GROUP * HEAD_DIM
KV_DIM = N_GROUPS * HEAD_DIM
CMP_LEN = 32
CMP_STRIDE = 16
SEL_BLOCK = 64
N_SELECT = 16
WINDOW = 512
N_BUCKETS = 32
MAX_EXACT = 16
MAX_DISTANCE = 2048
RMS_EPS = 1e-6
LN_EPS = 1e-5
FORCE_SCORE = 1e9

LANES = 128
SUBLANES = 8
QT = 128
KT = 256
CQ = 512
GATE_ROWS = 16
SCORE_LOOKAHEAD = 2
HALO = 32
MASKED = -1e30
MASK_TERM = -(2.0 ** 100)
M_INIT = -1e20
LOG2E = math.log2(math.e)
V_ROWS = HEAD_DIM + 16
PART_ROWS = HEAD_DIM + SUBLANES
N_FAR = MAX_DISTANCE // LANES + 1
N_WIN_BLOCKS = WINDOW // LANES + 1
VMEM_LIMIT = 56 * 1024 * 1024


def _params(n_axes):
    return pltpu.CompilerParams(dimension_semantics=("arbitrary",) * n_axes,
                                vmem_limit_bytes=VMEM_LIMIT)


def _resident(shape):
    nd = len(shape)
    return pl.BlockSpec(shape, lambda *_: (0,) * nd, pipeline_mode=pl.Buffered(1))


def _rms(x, g):
    return x * lax.rsqrt(jnp.mean(x * x, axis=-1, keepdims=True) + RMS_EPS) * g


def _dot(a, b):
    return jnp.dot(a, b, preferred_element_type=F32)


def _ffn_kernel(*refs, row_parts, half_tiles, epilogue):
    refs = list(refs)
    take = lambda n: [refs.pop(0) for _ in range(n)]
    (x_ref,) = take(1)
    proj = take(3) if half_tiles is not None else None
    g_ref, wg_ref, wu_ref, wd_ref = take(4)
    extra = take({"none": 0, "ple": 4, "ple_final": 5, "glu": 3}[epilogue])
    (o_ref,) = take(1)
    u_ref = take(1)[0] if epilogue == "glu" else None

    rows_per_part = x_ref.shape[1] // row_parts
    parts = [slice(k * rows_per_part, (k + 1) * rows_per_part) for k in range(row_parts)]
    hidden = []
    for rows in parts:
        x = x_ref[0, rows, :]
        if proj is not None:
            lo_ref, hi_ref, wo_ref = proj
            o = jnp.where(pl.program_id(1) < half_tiles, lo_ref[0, rows, :], hi_ref[0, rows, :])
            x = x + _dot(o, wo_ref[...])
            o_ref[0, rows, :] = x
        xn = _rms(x, g_ref[...]).astype(BF16)
        gate = _dot(xn, wg_ref[...])
        up = _dot(xn, wu_ref[...])
        hidden.append(((gate * jax.nn.sigmoid(gate)) * up).astype(BF16))
    for rows, h in zip(parts, hidden):
        x = x_ref[0, rows, :] if proj is None else o_ref[0, rows, :]
        y = x + 0.5 * _dot(h, wd_ref[...])
        if epilogue in ("ple", "ple_final"):
            p_ref, pg_ref, pwg_ref, pwi_ref = extra[:4]
            gate = jax.nn.sigmoid(_dot(_rms(y, pg_ref[...]).astype(BF16), pwg_ref[...]))
            y = y + gate * _dot(p_ref[0, rows, :].astype(BF16), pwi_ref[...])
            if epilogue == "ple_final":
                y = _rms(y, extra[4][...])
        elif epilogue == "glu":
            cg_ref, cw_ref, cb_ref = extra
            d = y.shape[-1]
            hn = _rms(y, cg_ref[...]).astype(BF16)
            a = _dot(hn, cw_ref[:, :d]) + cb_ref[:, :d]
            gt = _dot(hn, cw_ref[:, d:]) + cb_ref[:, d:]
            u_ref[0, rows, :] = a * jax.nn.sigmoid(gt)
        o_ref[0, rows, :] = y


def _ffn(x3, norm_g, wg, wu, wd, *, proj=None, ple=None, glu=None):
    b, s, d = x3.shape
    fused = proj is not None or ple is not None or glu is not None
    tm, row_parts = (512, 2) if fused else (1024, 4)
    tile = pl.BlockSpec((1, tm, d), lambda bi, i: (bi, i, 0))
    operands, in_specs = [x3], [tile]
    half_tiles = None
    if proj is not None:
        o_lo, o_hi, w_out = proj
        half_tiles = s // 2 // tm
        width = o_lo.shape[2]
        operands += [o_lo, o_hi, w_out]
        in_specs += [pl.BlockSpec((1, tm, width), lambda bi, i: (bi, jnp.minimum(i, half_tiles - 1), 0)),
                     pl.BlockSpec((1, tm, width), lambda bi, i: (bi, jnp.maximum(i - half_tiles, 0), 0)),
                     _resident(w_out.shape)]
    weights = [norm_g, wg, wu, wd]
    epilogue = "none"
    out_shape = [jax.ShapeDtypeStruct((b, s, d), F32)]
    if ple is not None:
        epilogue = "ple" if len(ple) == 4 else "ple_final"
        operands_after = [ple[0]]
        specs_after = [pl.BlockSpec((1, tm, ple[0].shape[2]), lambda bi, i: (bi, i, 0))]
        weights_after = list(ple[1:])
    elif glu is not None:
        epilogue = "glu"
        operands_after, specs_after, weights_after = [], [], list(glu)
        out_shape.append(jax.ShapeDtypeStruct((b, s, d), F32))
    else:
        operands_after, specs_after, weights_after = [], [], []
    operands += weights + operands_after + weights_after
    in_specs += [_resident(w.shape) for w in weights] + specs_after + [_resident(w.shape) for w in weights_after]
    out = pl.pallas_call(
        functools.partial(_ffn_kernel, row_parts=row_parts, half_tiles=half_tiles, epilogue=epilogue),
        out_shape=tuple(out_shape),
        grid=(b, s // tm),
        in_specs=in_specs,
        out_specs=tuple([tile] * len(out_shape)),
        compiler_params=_params(2),
        name="ffn",
    )(*operands)
    return out if glu is not None else out[0]


def _conv_out_kernel(u_ref, halo_ref, x_ref, wdw_ref, bdw_ref, lng_ref, lnb_ref, w2_ref, b2_ref,
                     o_ref, pad_ref, conv_ref, *, ts):
    first = pl.program_id(1) == 0
    pad_ref[0, 0:HALO, :] = jnp.where(first, 0.0, halo_ref[0])
    pad_ref[0, HALO:, :] = u_ref[0]
    shifted_rows = ts + HALO - SUBLANES
    for ph in range(1, SUBLANES):
        pad_ref[ph, 0:shifted_rows, :] = pad_ref[0, ph:ph + shifted_rows, :]
    off = HALO - (CONV_KERNEL - 1)
    d = u_ref.shape[-1]
    for c in range(d // LANES):
        cs = slice(c * LANES, (c + 1) * LANES)
        acc = jnp.zeros((ts, LANES), F32)
        for k in range(CONV_KERNEL):
            ph = (off + k) % SUBLANES
            base = off + k - ph
            acc = acc + wdw_ref[k:k + 1, cs] * pad_ref[ph, base:base + ts, cs]
        conv_ref[:, cs] = acc + bdw_ref[:, cs]
    u = conv_ref[...]
    mu = jnp.mean(u, axis=-1, keepdims=True)
    var = jnp.mean(jnp.square(u - mu), axis=-1, keepdims=True)
    y = (u - mu) * lax.rsqrt(var + LN_EPS) * lng_ref[...] + lnb_ref[...]
    y = y * jax.nn.sigmoid(y)
    o_ref[0] = x_ref[0] + _dot(y.astype(BF16), w2_ref[...]) + b2_ref[...]


def _conv_out(u3, x3, w_dw, b_dw, ln_g, ln_b, w_pw2, b_pw2, *, ts=256):
    b, s, d = u3.shape
    tile = pl.BlockSpec((1, ts, d), lambda bi, i: (bi, i, 0))
    halo = pl.BlockSpec((1, HALO, d), lambda bi, i: (bi, jnp.maximum(i * (ts // HALO) - 1, 0), 0))
    return pl.pallas_call(
        functools.partial(_conv_out_kernel, ts=ts),
        out_shape=jax.ShapeDtypeStruct((b, s, d), F32),
        grid=(b, s // ts),
        in_specs=[tile, halo, tile, _resident((CONV_KERNEL, d)), _resident((1, d)), _resident((1, d)),
                  _resident((1, d)), _resident((d, d)), _resident((1, d))],
        out_specs=tile,
        scratch_shapes=[pltpu.VMEM((SUBLANES, ts + HALO, d), F32), pltpu.VMEM((ts, d), F32)],
        compiler_params=_params(2),
        name="conv_out",
    )(u3, u3, x3, w_dw, b_dw, ln_g, ln_b, w_pw2, b_pw2)


def _nsa_in_kernel(x_ref, g_ref, wq_ref, wkv_ref, wgt_ref,
                   qt_ref, kc_ref, vc_ref, ksa_ref, kwa_ref, vst_ref, vwt_ref, gt_ref, *, tm):
    hn = _rms(x_ref[0], g_ref[...]).astype(BF16)
    q = _dot(hn, wq_ref[...]) * (HEAD_DIM ** -0.5 * LOG2E)
    heads_per_chunk = LANES // HEAD_DIM

    def transposed_heads(cols, chunk):
        t = cols[:, chunk * LANES:(chunk + 1) * LANES].T
        return [t[hh * HEAD_DIM:(hh + 1) * HEAD_DIM] for hh in range(heads_per_chunk)]

    for chunk in range(N_HEADS // heads_per_chunk):
        for hh, qt in enumerate(transposed_heads(q, chunk)):
            g, r = divmod(chunk * heads_per_chunk + hh, HEADS_PER_GROUP)
            qt_ref[0, g, r] = qt.astype(BF16)

    gates = jax.nn.sigmoid(_dot(hn, wgt_ref[...]))
    for g in range(N_GROUPS):
        gt_ref[0, g] = gates[:, g * LANES:(g + 1) * LANES].T[:GATE_ROWS]

    kv = _dot(hn, wkv_ref[...])
    pos = pl.program_id(1) * tm + lax.broadcasted_iota(jnp.int32, (tm, SEL_BLOCK), 0)
    blk = lax.broadcasted_iota(jnp.int32, (tm, SEL_BLOCK), 1)
    onehot = jnp.where(pos // SEL_BLOCK == blk, 1.0, 0.0).astype(BF16)
    nothing = jnp.zeros((tm, SEL_BLOCK), BF16)
    for g in range(N_GROUPS):
        def part(j):
            lo = j * KV_DIM + g * HEAD_DIM
            return kv[:, lo:lo + HEAD_DIM]
        kc_ref[0, g] = part(0)
        vc_ref[0, g] = part(1)
        ksa_ref[0, g] = jnp.concatenate([part(2).astype(BF16), onehot], axis=1)
        kwa_ref[0, g] = jnp.concatenate([part(4).astype(BF16), nothing], axis=1)
    v_sel = kv[:, 3 * KV_DIM:4 * KV_DIM]
    v_win = kv[:, 5 * KV_DIM:6 * KV_DIM]
    for chunk in range(N_GROUPS // heads_per_chunk):
        for hh, (vs, vw) in enumerate(zip(transposed_heads(v_sel, chunk), transposed_heads(v_win, chunk))):
            g = chunk * heads_per_chunk + hh
            for j in range(tm // KT):
                vst_ref[0, g, j, 0:HEAD_DIM, :] = vs[:, j * KT:(j + 1) * KT].astype(BF16)
                vst_ref[0, g, j, HEAD_DIM:, :] = jnp.ones((V_ROWS - HEAD_DIM, KT), BF16)
            for j in range(tm // LANES):
                vwt_ref[0, g, j, 0:HEAD_DIM, :] = vw[:, j * LANES:(j + 1) * LANES].astype(BF16)
                vwt_ref[0, g, j, HEAD_DIM:, :] = jnp.ones((V_ROWS - HEAD_DIM, LANES), BF16)


def _nsa_in(x3, norm_g, wq, wkv, wgt, *, tm=512):
    b, s, d = x3.shape
    per_group = lambda width: pl.BlockSpec((1, N_GROUPS, tm, width), lambda bi, i: (bi, 0, i, 0))
    kv_shape = lambda width, dt: jax.ShapeDtypeStruct((b, N_GROUPS, s, width), dt)
    return pl.pallas_call(
        functools.partial(_nsa_in_kernel, tm=tm),
        out_shape=(jax.ShapeDtypeStruct((b, N_GROUPS, HEADS_PER_GROUP, HEAD_DIM, s), BF16),
                   kv_shape(HEAD_DIM, F32), kv_shape(HEAD_DIM, F32),
                   kv_shape(2 * HEAD_DIM, BF16), kv_shape(2 * HEAD_DIM, BF16),
                   jax.ShapeDtypeStruct((b, N_GROUPS, s // KT, V_ROWS, KT), BF16),
                   jax.ShapeDtypeStruct((b, N_GROUPS, s // LANES, V_ROWS, LANES), BF16),
                   jax.ShapeDtypeStruct((b, N_GROUPS, GATE_ROWS, s), F32)),
        grid=(b, s // tm),
        in_specs=[pl.BlockSpec((1, tm, d), lambda bi, i: (bi, i, 0)), _resident((1, d)),
                  _resident(wq.shape), _resident(wkv.shape), _resident(wgt.shape)],
        out_specs=(pl.BlockSpec((1, N_GROUPS, HEADS_PER_GROUP, HEAD_DIM, tm), lambda bi, i: (bi, 0, 0, 0, i)),
                   per_group(HEAD_DIM), per_group(HEAD_DIM), per_group(2 * HEAD_DIM), per_group(2 * HEAD_DIM),
                   pl.BlockSpec((1, N_GROUPS, tm // KT, V_ROWS, KT), lambda bi, i: (bi, 0, i, 0, 0)),
                   pl.BlockSpec((1, N_GROUPS, tm // LANES, V_ROWS, LANES), lambda bi, i: (bi, 0, i, 0, 0)),
                   pl.BlockSpec((1, N_GROUPS, GATE_ROWS, tm), lambda bi, i: (bi, 0, 0, i))),
        compiler_params=_params(2),
        name="nsa_in",
    )(x3, norm_g, wq, wkv, wgt)


def _compress_kernel(kx_ref, vx_ref, pk_ref, pv_ref, wk1_ref, wk2_ref, wv1_ref, wv2_ref, kc_ref, vct_ref, *, nc):
    def mlp(x_ref, p_ref, w1_ref, w2_ref):
        first = second = None
        for l in range(CMP_STRIDE):
            rows = x_ref[0, 0, pl.ds(l, nc, stride=CMP_STRIDE), :]
            lo = _dot((rows + p_ref[l:l + 1, :]).astype(BF16), w1_ref[l])
            hi = _dot((rows + p_ref[CMP_STRIDE + l:CMP_STRIDE + l + 1, :]).astype(BF16), w1_ref[CMP_STRIDE + l])
            first = lo if first is None else first + lo
            second = hi if second is None else second + hi
        hidden = first + pltpu.roll(second, nc - 1, 0)
        return _dot(jax.nn.gelu(hidden).astype(BF16), w2_ref[...])
    kc_ref[0, 0] = mlp(kx_ref, pk_ref, wk1_ref, wk2_ref)[:, :HEAD_DIM].astype(BF16)
    vct_ref[0, 0] = mlp(vx_ref, pv_ref, wv1_ref, wv2_ref).T[:HEAD_DIM].astype(BF16)


def _compress(kx, vx, pos_k, pos_v, wk1, wk2, wv1, wv2):
    b, g, s, width = kx.shape
    nc = s // CMP_STRIDE
    blk = pl.BlockSpec((1, 1, s, width), lambda bi, gi: (bi, gi, 0, 0))
    return pl.pallas_call(
        functools.partial(_compress_kernel, nc=nc),
        out_shape=(jax.ShapeDtypeStruct((b, g, nc, HEAD_DIM), BF16),
                   jax.ShapeDtypeStruct((b, g, HEAD_DIM, nc), BF16)),
        grid=(b, g),
        in_specs=[blk, blk, _resident(pos_k.shape), _resident(pos_v.shape), _resident(wk1.shape),
                  _resident(wk2.shape), _resident(wv1.shape), _resident(wv2.shape)],
        out_specs=(pl.BlockSpec((1, 1, nc, HEAD_DIM), lambda bi, gi: (bi, gi, 0, 0)),
                   pl.BlockSpec((1, 1, HEAD_DIM, nc), lambda bi, gi: (bi, gi, 0, 0))),
        compiler_params=_params(2),
        name="compress",
    )(kx, vx, pos_k, pos_v, wk1, wk2, wv1, wv2)


def _t5_bucket(dist):
    n = jnp.maximum(dist, 0)
    nf = jnp.maximum(n, 1).astype(F32)
    large = MAX_EXACT + (jnp.log(nf / MAX_EXACT) / math.log(MAX_DISTANCE / MAX_EXACT)
                         * (N_BUCKETS - MAX_EXACT)).astype(jnp.int32)
    large = jnp.minimum(large, N_BUCKETS - 1)
    return jnp.where(n < MAX_EXACT, n, large)


def _bias_table_kernel(rb_ref, bucket_ref, o_ref):
    h = pl.program_id(0)
    bucket = bucket_ref[...]
    out = jnp.full(bucket.shape, MASKED, F32)
    for bkt in range(N_BUCKETS):
        out = jnp.where(bucket == bkt, rb_ref[bkt, h] * LOG2E, out)
    o_ref[0] = out


def _bias_table(rel_bias, bucket, *, tr):
    rows, lanes = bucket.shape
    return pl.pallas_call(
        _bias_table_kernel,
        out_shape=jax.ShapeDtypeStruct((N_HEADS, rows, lanes), F32),
        grid=(N_HEADS, rows // tr),
        in_specs=[pl.BlockSpec(memory_space=pltpu.SMEM), pl.BlockSpec((tr, lanes), lambda h, i: (i, 0))],
        out_specs=pl.BlockSpec((1, tr, lanes), lambda h, i: (h, i, 0)),
        compiler_params=_params(2),
        name="bias_table",
    )(rel_bias, bucket)


def _bias_tables(rel_bias, s):
    nc = s // CMP_STRIDE
    key = jnp.arange(LANES)[:, None]
    qry = jnp.arange(LANES)[None, :]

    def toeplitz_blocks(offsets, max_dist):
        dist = LANES * offsets[:, None, None] + qry[None] - key[None]
        valid = (dist >= 0) & (dist < max_dist)
        return jnp.where(valid, _t5_bucket(dist), -1).astype(jnp.int32).reshape(-1, LANES)

    sel_bucket = toeplitz_blocks(jnp.arange(-1, N_FAR + 1), 1 << 30)
    win_bucket = toeplitz_blocks(jnp.arange(-1, N_WIN_BLOCKS), WINDOW)
    cmp_dist = jnp.arange(s)[None, :] - (jnp.arange(nc)[:, None] * CMP_STRIDE + CMP_LEN - 1)
    cmp_bucket = jnp.where(cmp_dist >= 0, _t5_bucket(cmp_dist), -1).astype(jnp.int32)
    t_sel = _bias_table(rel_bias, sel_bucket, tr=sel_bucket.shape[0]).reshape(N_HEADS, N_FAR + 2, LANES, LANES)
    t_win = _bias_table(rel_bias, win_bucket, tr=win_bucket.shape[0]).reshape(N_HEADS, N_WIN_BLOCKS + 1, LANES, LANES)
    t_cmp = _bias_table(rel_bias, cmp_bucket, tr=nc)
    return t_sel, t_win, t_cmp


def _sublane_fold(x, op):
    parts = [x[k * SUBLANES:(k + 1) * SUBLANES] for k in range(x.shape[0] // SUBLANES)]
    while len(parts) > 1:
        parts = [op(parts[k], parts[k + 1]) for k in range(0, len(parts) - 1, 2)] + parts[len(parts) & ~1:]
    return parts[0]


def _cmp_kernel(qt_ref, kc_ref, vct_ref, tc_ref, ovt_ref, oct_ref, mtt_ref, *, n_sel, k_sel):
    heads = range(HEADS_PER_GROUP)
    q_t = jnp.concatenate([qt_ref[0, 0, r] for r in heads], axis=1)
    bias = jnp.concatenate([tc_ref[r] for r in heads], axis=1)
    s = _dot(kc_ref[0, 0], q_t) + bias
    m = jnp.maximum(jnp.max(s, axis=0, keepdims=True), M_INIT)
    e = jnp.exp2(s - m)
    pr = e * (1.0 / jnp.maximum(jnp.sum(e, axis=0, keepdims=True), 1e-30))
    o_t = _dot(vct_ref[0, 0], pr.astype(BF16))
    for r in heads:
        oct_ref[0, 0, r] = o_t[:, r * CQ:(r + 1) * CQ]

    psum = pr[:, 0:CQ]
    for r in heads[1:]:
        psum = psum + pr[:, r * CQ:(r + 1) * CQ]
    hi = psum.astype(BF16)
    rem = psum - hi.astype(F32)
    mid = rem.astype(BF16)
    lo = (rem - mid.astype(F32)).astype(BF16)
    ov_t = ovt_ref[...]
    imp = _dot(ov_t, hi) + _dot(ov_t, mid) + _dot(ov_t, lo)

    t = pl.program_id(0) * CQ + lax.broadcasted_iota(jnp.int32, (LANES, CQ), 1)
    j = lax.broadcasted_iota(jnp.int32, (LANES, CQ), 0)
    cur = t // SEL_BLOCK
    forced = (j == 0) | (j == cur) | (j == cur - 1)
    score = jnp.where(forced, FORCE_SCORE, jnp.where(j * SEL_BLOCK <= t, imp, -1.0))

    terms = []
    for v in range(n_sel // SUBLANES):
        mine = score[v * SUBLANES:(v + 1) * SUBLANES]
        jj = v * SUBLANES + lax.broadcasted_iota(jnp.int32, (SUBLANES, CQ), 0)
        rank = jnp.zeros((SUBLANES, CQ), jnp.int32)
        for i in range(n_sel):
            other = score[i:i + 1]
            if i < v * SUBLANES:
                beats = other >= mine
            elif i >= (v + 1) * SUBLANES:
                beats = other > mine
            else:
                beats = (other > mine) | ((other == mine) & (i < jj))
            rank = rank + beats.astype(jnp.int32)
        terms.append(jnp.where(rank < k_sel, 0.0, MASK_TERM))
    if n_sel < SEL_BLOCK:
        terms.append(jnp.zeros((SEL_BLOCK - n_sel, CQ), F32))
    mtt_ref[0, 0] = jnp.concatenate(terms, axis=0).astype(BF16)


def _cmp_branch(q_t, kc, vc_t, t_cmp, overlap_t):
    b, _, _, _, s = q_t.shape
    nc = kc.shape[2]
    n_sel = s // SEL_BLOCK
    qspec = pl.BlockSpec((1, 1, HEADS_PER_GROUP, HEAD_DIM, CQ), lambda qi, gi, bi: (bi, gi, 0, 0, qi))
    return pl.pallas_call(
        functools.partial(_cmp_kernel, n_sel=n_sel, k_sel=min(N_SELECT, n_sel)),
        out_shape=(jax.ShapeDtypeStruct((b, N_GROUPS, HEADS_PER_GROUP, HEAD_DIM, s), F32),
                   jax.ShapeDtypeStruct((b, N_GROUPS, SEL_BLOCK, s), BF16)),
        grid=(s // CQ, N_GROUPS, b),
        in_specs=[qspec,
                  pl.BlockSpec((1, 1, nc, HEAD_DIM), lambda qi, gi, bi: (bi, gi, 0, 0)),
                  pl.BlockSpec((1, 1, HEAD_DIM, nc), lambda qi, gi, bi: (bi, gi, 0, 0)),
                  pl.BlockSpec((HEADS_PER_GROUP, nc, CQ), lambda qi, gi, bi: (gi, 0, qi)),
                  _resident(overlap_t.shape)],
        out_specs=(qspec, pl.BlockSpec((1, 1, SEL_BLOCK, CQ), lambda qi, gi, bi: (bi, gi, 0, qi))),
        compiler_params=_params(3),
        name="cmp_branch",
    )(q_t, kc, vc_t, t_cmp, overlap_t)


def _selwin_kernel(qta_ref, qtb_ref, mta_ref, mtb_ref, ksa_ref, kwa_ref, vst_ref, vwt_ref, tsel_ref, twin_ref,
                   oca_ref, ocb_ref, gta_ref, gtb_ref, oa_ref, ob_ref,
                   qa_scr, part_scr, win_scr, wsc_scr, ssc_scr, *, n_q):
    i = pl.program_id(2)
    heads = range(HEADS_PER_GROUP)
    cols = HEADS_PER_GROUP * QT
    n_slots = part_scr.shape[0]
    blocks_per_tile = KT // LANES
    q_tiles = (i, n_q - 1 - i)
    n_first = i // blocks_per_tile + 1

    for w, (qt_ref, mt_ref) in enumerate(((qta_ref, mta_ref), (qtb_ref, mtb_ref))):
        qa_scr[w] = jnp.concatenate(
            [jnp.concatenate([qt_ref[0, 0, r], mt_ref[0, 0]], axis=0) for r in heads], axis=1)

    def slot_info(slot):
        second = slot >= n_first
        w = second.astype(jnp.int32)
        kt = jnp.where(second, slot - n_first, slot)
        tq = jnp.where(second, q_tiles[1], q_tiles[0])
        return second, w, kt, tq

    def normalised(aug):
        return aug[:HEAD_DIM] * (1.0 / jnp.maximum(aug[HEAD_DIM:HEAD_DIM + 1], 1e-30))

    def win_first_block(w):
        return jnp.maximum(q_tiles[w] - (N_WIN_BLOCKS - 1), 0)

    def biased_scores(raw, table_ref, block_index, out_ref):
        folded = []
        for r in heads:
            cs = slice(r * QT, (r + 1) * QT)
            fold = None
            for mb in range(raw.shape[0] // LANES):
                rs = slice(mb * LANES, (mb + 1) * LANES)
                sb = raw[rs, cs] + table_ref[r, block_index(mb)]
                out_ref[rs, cs] = sb
                part = _sublane_fold(sb, jnp.maximum)
                fold = part if fold is None else jnp.maximum(fold, part)
            folded.append(fold)
        return jnp.maximum(jnp.max(jnp.concatenate(folded, axis=1), axis=0, keepdims=True), M_INIT)

    def win_scores(w):
        first_blk = win_first_block(w)
        w0 = pl.multiple_of(first_blk * LANES, LANES)
        raw = _dot(kwa_ref[0, 0, pl.ds(w0, N_WIN_BLOCKS * LANES), :], qa_scr[w])

        def block_index(mb):
            a = q_tiles[w] - (first_blk + mb)
            return jnp.where(a < 0, 0, a + 1)
        return biased_scores(raw, twin_ref, block_index, wsc_scr.at[w])

    def win_finish(w, m_w):
        first_blk = win_first_block(w)
        win = None
        for mb in range(N_WIN_BLOCKS):
            e_w = jnp.exp2(wsc_scr[w, mb * LANES:(mb + 1) * LANES, :] - m_w).astype(BF16)
            term = _dot(vwt_ref[0, 0, first_blk + mb], e_w)
            win = term if win is None else win + term
        win_scr[w] = normalised(win)

    m_slots = {}
    ring = ssc_scr.shape[0]

    def sel_scores(slot):
        second, w, kt, tq = slot_info(slot)
        k0 = pl.multiple_of(kt * KT, KT)
        raw = _dot(ksa_ref[0, 0, pl.ds(k0, KT), :], qa_scr[w])
        block_index = lambda mb: jnp.clip(tq - (kt * blocks_per_tile + mb), -1, N_FAR) + 1
        m_slots[slot] = biased_scores(raw, tsel_ref, block_index, ssc_scr.at[slot % ring])
        return m_slots[slot]

    def sel_finish(slot, m_slot):
        kt = slot_info(slot)[2]
        p = jnp.exp2(ssc_scr[slot % ring] - m_slot).astype(BF16)
        part_scr[slot] = _dot(vst_ref[0, 0, kt], p)[:PART_ROWS]

    items = [(sel_scores, sel_finish, k) for k in range(n_slots)] + [(win_scores, win_finish, w) for w in range(2)]

    def merge_slots():
        max_first = (n_q // 2 - 1) // blocks_per_tile + 1
        seconds = [slot_info(slot)[0] for slot in range(n_slots)]
        m_fin = [functools.reduce(jnp.maximum, [jnp.where(seconds[k], M_INIT, m_slots[k]) for k in range(max_first)]),
                 functools.reduce(jnp.maximum, [jnp.where(seconds[k], m_slots[k], M_INIT) if k < max_first
                                                else m_slots[k] for k in range(1, n_slots)])]
        merged = []
        for w in range(2):
            acc = None
            for slot in (range(max_first) if w == 0 else range(1, n_slots)):
                weight = jnp.exp2(m_slots[slot] - m_fin[w])
                if 0 < slot < max_first:
                    owned = seconds[slot] if w == 1 else jnp.logical_not(seconds[slot])
                    weight = jnp.where(owned, weight, 0.0)
                term = part_scr[slot] * weight
                acc = term if acc is None else acc + term
            merged.append(normalised(acc))
        return merged

    pending = {}
    o_sels = None
    for n in range(len(items) + SCORE_LOOKAHEAD):
        if n < len(items):
            scores, _, arg = items[n]
            pending[n] = scores(arg)
        if n >= SCORE_LOOKAHEAD:
            _, finish, arg = items[n - SCORE_LOOKAHEAD]
            finish(arg, pending.pop(n - SCORE_LOOKAHEAD))
            if n - SCORE_LOOKAHEAD == n_slots - 1:
                o_sels = merge_slots()

    for w, (oc_ref, gt_ref, o_ref) in enumerate(((oca_ref, gta_ref, oa_ref), (ocb_ref, gtb_ref, ob_ref))):
        o_sel = o_sels[w]
        o_win = win_scr[w]

        gates = gt_ref[0, 0]
        mixed = []
        for r in heads:
            cs = slice(r * QT, (r + 1) * QT)
            g_c, g_s, g_w = (gates[3 * r + j:3 * r + j + 1] for j in range(3))
            mixed.append(g_c * oc_ref[0, 0, r] + g_s * o_sel[:, cs] + g_w * o_win[:, cs])
        pairs = [jnp.concatenate(mixed[k:k + 2], axis=0).T for k in range(0, HEADS_PER_GROUP, 2)]
        o_ref[0] = jnp.concatenate(pairs, axis=1).astype(BF16)


def _selwin_branch(q_t, mterm_t, ksa, kwa, vs_t, vw_t, t_sel, t_win, o_cmp_t, gates_t):
    b, _, _, _, s = q_t.shape
    n_q = s // QT
    half = n_q // 2
    n_slots = half + 1
    cols = HEADS_PER_GROUP * QT
    fwd = lambda gi, bi, i: i
    bwd = lambda gi, bi, i: n_q - 1 - i
    qspec = lambda pick: pl.BlockSpec((1, 1, HEADS_PER_GROUP, HEAD_DIM, QT),
                                      lambda gi, bi, i: (bi, gi, 0, 0, pick(gi, bi, i)))
    rowspec = lambda rows, pick: pl.BlockSpec((1, 1, rows, QT), lambda gi, bi, i: (bi, gi, 0, pick(gi, bi, i)))
    whole = lambda arr: pl.BlockSpec((1, 1) + arr.shape[2:], lambda gi, bi, i: (bi, gi) + (0,) * (arr.ndim - 2))
    tab = lambda n: pl.BlockSpec((HEADS_PER_GROUP, n, LANES, LANES), lambda gi, bi, i: (gi, 0, 0, 0))
    out_shape = jax.ShapeDtypeStruct((b, s // 2, N_HEADS * HEAD_DIM), BF16)
    return pl.pallas_call(
        functools.partial(_selwin_kernel, n_q=n_q),
        out_shape=(out_shape, out_shape),
        grid=(N_GROUPS, b, half),
        in_specs=[qspec(fwd), qspec(bwd), rowspec(SEL_BLOCK, fwd), rowspec(SEL_BLOCK, bwd),
                  whole(ksa), whole(kwa), whole(vs_t), whole(vw_t),
                  tab(t_sel.shape[1]), tab(t_win.shape[1]),
                  qspec(fwd), qspec(bwd), rowspec(GATE_ROWS, fwd), rowspec(GATE_ROWS, bwd)],
        out_specs=(pl.BlockSpec((1, QT, GROUP_DIM), lambda gi, bi, i: (bi, i, gi)),
                   pl.BlockSpec((1, QT, GROUP_DIM), lambda gi, bi, i: (bi, half - 1 - i, gi))),
        scratch_shapes=[pltpu.VMEM((2, 2 * HEAD_DIM, cols), BF16),
                        pltpu.VMEM((n_slots, PART_ROWS, cols), F32),
                        pltpu.VMEM((2, HEAD_DIM, cols), F32),
                        pltpu.VMEM((2, N_WIN_BLOCKS * LANES, cols), F32),
                        pltpu.VMEM((SCORE_LOOKAHEAD + 1, KT, cols), F32)],
        compiler_params=_params(3),
        name="selwin_branch",
    )(q_t, q_t, mterm_t, mterm_t, ksa, kwa, vs_t, vw_t, t_sel, t_win, o_cmp_t, o_cmp_t, gates_t, gates_t)


def _nsa_mixer(x3, norm_g, w_in, pos_k, pos_v, wk1, wk2, wv1, wv2, tables):
    b, s, d = x3.shape
    n_q = s // QT
    assert KT == 2 * QT and n_q % 2 == 0 and s % CQ == 0 and n_q >= N_WIN_BLOCKS and s // SEL_BLOCK <= SEL_BLOCK
    t_sel, t_win, t_cmp = tables
    q_dim = N_HEADS * HEAD_DIM
    wq = w_in[:, :q_dim].astype(BF16)
    wkv = w_in[:, q_dim:q_dim + 6 * KV_DIM].astype(BF16)
    wg = w_in[:, q_dim + 6 * KV_DIM:].reshape(d, N_GROUPS, 3 * HEADS_PER_GROUP)
    wg = jnp.pad(wg, ((0, 0), (0, 0), (0, LANES - 3 * HEADS_PER_GROUP))).reshape(d, N_GROUPS * LANES).astype(BF16)
    q_t, kc_raw, vc_raw, ksa, kwa, vs_t, vw_t, gates_t = _nsa_in(x3, norm_g, wq, wkv, wg)

    nc = s // CMP_STRIDE
    per_row = lambda w: w.reshape(CMP_LEN, HEAD_DIM, -1).astype(BF16)
    widen = lambda w: jnp.pad(w, ((0, 0), (0, LANES - HEAD_DIM))).astype(BF16)
    kc, vc_t = _compress(kc_raw, vc_raw, pos_k, pos_v, per_row(wk1), widen(wk2), per_row(wv1), widen(wv2))

    n_cmp = (s - CMP_LEN) // CMP_STRIDE + 1
    c_start = jnp.arange(nc)[None, :] * CMP_STRIDE
    n_start = jnp.arange(LANES)[:, None] * SEL_BLOCK
    overlap_t = ((c_start < n_start + SEL_BLOCK) & (c_start + CMP_LEN > n_start)
                 & (jnp.arange(nc)[None, :] < n_cmp) & (jnp.arange(LANES)[:, None] < s // SEL_BLOCK))
    o_cmp_t, mterm_t = _cmp_branch(q_t, kc, vc_t, t_cmp, overlap_t.astype(BF16))
    return _selwin_branch(q_t, mterm_t, ksa, kwa, vs_t, vw_t, t_sel, t_win, o_cmp_t, gates_t)


def kernel(x, p, rel_bias, ffn1_norm, ffn1_w_gate, ffn1_w_up, ffn1_w_down, mix_norm, ffn2_norm, ffn2_w_gate, ffn2_w_up, ffn2_w_down, ple_norm, ple_w_gate, ple_w_in, conv_w_pw1, conv_b_pw1, conv_w_dw, conv_b_dw, conv_ln_g, conv_ln_b, conv_w_pw2, conv_b_pw2, nsa_w_in, nsa_cmp_pos_k, nsa_cmp_pos_v, nsa_cmp_wk1, nsa_cmp_wk2, nsa_cmp_wv1, nsa_cmp_wv2, nsa_w_out, final_norm):
    b, s, d = x.shape
    depth = ffn1_norm.shape[0]
    row = lambda v: v.reshape(1, -1)
    tables = _bias_tables(rel_bias, s)
    bf = lambda w: w.astype(BF16)
    for i in range(depth):
        j = i // 2
        ffn1 = (row(ffn1_norm[i]), bf(ffn1_w_gate[i]), bf(ffn1_w_up[i]), bf(ffn1_w_down[i]))
        ffn2 = (row(ffn2_norm[i]), bf(ffn2_w_gate[i]), bf(ffn2_w_up[i]), bf(ffn2_w_down[i]))
        ple = (p[i], row(ple_norm[i]), bf(ple_w_gate[i]), bf(ple_w_in[i]))
        if i == depth - 1:
            ple += (row(final_norm),)
        if i % 2 == 0:
            x, u = _ffn(x, *ffn1, glu=(row(mix_norm[i]), bf(conv_w_pw1[j]), row(conv_b_pw1[j])))
            x = _conv_out(u, x, conv_w_dw[j], row(conv_b_dw[j]), row(conv_ln_g[j]), row(conv_ln_b[j]),
                          bf(conv_w_pw2[j]), row(conv_b_pw2[j]))
            x = _ffn(x, *ffn2, ple=ple)
        else:
            x = _ffn(x, *ffn1)
            o_lo, o_hi = _nsa_mixer(x, row(mix_norm[i]), nsa_w_in[j], nsa_cmp_pos_k[j], nsa_cmp_pos_v[j],
                                    nsa_cmp_wk1[j], nsa_cmp_wk2[j], nsa_cmp_wv1[j], nsa_cmp_wv2[j], tables)
            x = _ffn(x, *ffn2, proj=(o_lo, o_hi, bf(nsa_w_out[j])), ple=ple)
    return x
```

```python
import functools
import math

import jax
import jax.numpy as jnp
from jax import lax
from jax.experimental import pallas as pl
from jax.experimental.pallas import tpu as pltpu

F32 = jnp.float32
BF16 = jnp.bfloat16

D_MODEL = 1024
D_FF = 2816
CONV_KERNEL = 31
HEAD_DIM = 64
N_HEADS = 16
N_GROUPS = 4
HEADS_PER_GROUP = 4
GROUP_DIM = HEADS_PER_GROUP * HEAD_DIM
KV_DIM = N_GROUPS * HEAD_DIM
CMP_LEN = 32
CMP_STRIDE = 16
SEL_BLOCK = 64
N_SELECT = 16
WINDOW = 512
N_BUCKETS = 32
MAX_EXACT = 16
MAX_DISTANCE = 2048
RMS_EPS = 1e-6
LN_EPS = 1e-5
FORCE_SCORE = 1e9

LANES = 128
SUBLANES = 8
QT = 128
KT = 256
QB = QT // LANES
KB = KT // LANES
CQ = 512
CMP_PARTS = 4
GATE_ROWS = 16
SCORE_LOOKAHEAD = 2
HALO = 32
MASKED = -1e30
MASK_TERM = -(2.0 ** 100)
M_INIT = -1e20
LOG2E = math.log2(math.e)
V_ROWS = HEAD_DIM + 16
PART_ROWS = HEAD_DIM + SUBLANES
N_FAR = MAX_DISTANCE // LANES + 1
N_WIN_BLOCKS = WINDOW // LANES + 1
WIN_TILE_BLOCKS = WINDOW // LANES + QB
VMEM_LIMIT = 56 * 1024 * 1024


def _params(n_axes):
    return pltpu.CompilerParams(dimension_semantics=("arbitrary",) * n_axes,
                                vmem_limit_bytes=VMEM_LIMIT)


def _resident(shape):
    nd = len(shape)
    return pl.BlockSpec(shape, lambda *_: (0,) * nd, pipeline_mode=pl.Buffered(1))


def _rms(x, g):
    return x * lax.rsqrt(jnp.mean(x * x, axis=-1, keepdims=True) + RMS_EPS) * g


def _dot(a, b):
    return jnp.dot(a, b, preferred_element_type=F32)


def _ffn_kernel(*refs, row_parts, half_tiles, epilogue):
    refs = list(refs)
    take = lambda n: [refs.pop(0) for _ in range(n)]
    (x_ref,) = take(1)
    proj = take(3) if half_tiles is not None else None
    g_ref, wg_ref, wu_ref, wd_ref = take(4)
    extra = take({"none": 0, "ple": 4, "ple_final": 5, "glu": 3}[epilogue])
    (o_ref,) = take(1)
    u_ref = take(1)[0] if epilogue == "glu" else None

    rows_per_part = x_ref.shape[1] // row_parts
    parts = [slice(k * rows_per_part, (k + 1) * rows_per_part) for k in range(row_parts)]
    hidden = []
    for rows in parts:
        x = x_ref[0, rows, :]
        if proj is not None:
            lo_ref, hi_ref, wo_ref = proj
            o = jnp.where(pl.program_id(1) < half_tiles, lo_ref[0, rows, :], hi_ref[0, rows, :])
            x = x + _dot(o, wo_ref[...])
            o_ref[0, rows, :] = x
        xn = _rms(x, g_ref[...]).astype(BF16)
        gate = _dot(xn, wg_ref[...])
        up = _dot(xn, wu_ref[...])
        hidden.append(((gate * jax.nn.sigmoid(gate)) * up).astype(BF16))
    for rows, h in zip(parts, hidden):
        x = x_ref[0, rows, :] if proj is None else o_ref[0, rows, :]
        y = x + 0.5 * _dot(h, wd_ref[...])
        if epilogue in ("ple", "ple_final"):
            p_ref, pg_ref, pwg_ref, pwi_ref = extra[:4]
            gate = jax.nn.sigmoid(_dot(_rms(y, pg_ref[...]).astype(BF16), pwg_ref[...]))
            y = y + gate * _dot(p_ref[0, rows, :].astype(BF16), pwi_ref[...])
            if epilogue == "ple_final":
                y = _rms(y, extra[4][...])
        elif epilogue == "glu":
            cg_ref, cw_ref, cb_ref = extra
            d = y.shape[-1]
            hn = _rms(y, cg_ref[...]).astype(BF16)
            a = _dot(hn, cw_ref[:, :d]) + cb_ref[:, :d]
            gt = _dot(hn, cw_ref[:, d:]) + cb_ref[:, d:]
            u_ref[0, rows, :] = a * jax.nn.sigmoid(gt)
        o_ref[0, rows, :] = y


def _ffn(x3, norm_g, wg, wu, wd, *, proj=None, ple=None, glu=None):
    b, s, d = x3.shape
    fused = proj is not None or ple is not None or glu is not None
    tm, row_parts = (512, 2) if fused else (1024, 4)
    tile = pl.BlockSpec((1, tm, d), lambda bi, i: (bi, i, 0))
    operands, in_specs = [x3], [tile]
    half_tiles = None
    if proj is not None:
        o_lo, o_hi, w_out = proj
        half_tiles = s // 2 // tm
        width = o_lo.shape[2]
        operands += [o_lo, o_hi, w_out]
        in_specs += [pl.BlockSpec((1, tm, width), lambda bi, i: (bi, jnp.minimum(i, half_tiles - 1), 0)),
                     pl.BlockSpec((1, tm, width), lambda bi, i: (bi, jnp.maximum(i - half_tiles, 0), 0)),
                     _resident(w_out.shape)]
    weights = [norm_g, wg, wu, wd]
    epilogue = "none"
    out_shape = [jax.ShapeDtypeStruct((b, s, d), F32)]
    if ple is not None:
        epilogue = "ple" if len(ple) == 4 else "ple_final"
        p_all, layer = ple[0]
        operands_after = [p_all]
        specs_after = [pl.BlockSpec((None, 1, tm, p_all.shape[3]), lambda bi, i: (layer, bi, i, 0))]
        weights_after = list(ple[1:])
    elif glu is not None:
        epilogue = "glu"
        operands_after, specs_after, weights_after = [], [], list(glu)
        out_shape.append(jax.ShapeDtypeStruct((b, s, d), F32))
    else:
        operands_after, specs_after, weights_after = [], [], []
    operands += weights + operands_after + weights_after
    in_specs += [_resident(w.shape) for w in weights] + specs_after + [_resident(w.shape) for w in weights_after]
    out = pl.pallas_call(
        functools.partial(_ffn_kernel, row_parts=row_parts, half_tiles=half_tiles, epilogue=epilogue),
        out_shape=tuple(out_shape),
        grid=(b, s // tm),
        in_specs=in_specs,
        out_specs=tuple([tile] * len(out_shape)),
        compiler_params=_params(2),
        name="ffn",
    )(*operands)
    return out if glu is not None else out[0]


def _conv_out_kernel(u_ref, halo_ref, x_ref, wdw_ref, bdw_ref, lng_ref, lnb_ref, w2_ref, b2_ref,
                     o_ref, pad_ref, conv_ref, *, ts):
    first = pl.program_id(1) == 0
    pad_ref[0, 0:HALO, :] = jnp.where(first, 0.0, halo_ref[0])
    pad_ref[0, HALO:, :] = u_ref[0]
    shifted_rows = ts + HALO - SUBLANES
    for ph in range(1, SUBLANES):
        pad_ref[ph, 0:shifted_rows, :] = pad_ref[0, ph:ph + shifted_rows, :]
    off = HALO - (CONV_KERNEL - 1)
    d = u_ref.shape[-1]
    for c in range(d // LANES):
        cs = slice(c * LANES, (c + 1) * LANES)
        acc = jnp.zeros((ts, LANES), F32)
        for k in range(CONV_KERNEL):
            ph = (off + k) % SUBLANES
            base = off + k - ph
            acc = acc + wdw_ref[k:k + 1, cs] * pad_ref[ph, base:base + ts, cs]
        conv_ref[:, cs] = acc + bdw_ref[:, cs]
    u = conv_ref[...]
    mu = jnp.mean(u, axis=-1, keepdims=True)
    var = jnp.mean(jnp.square(u - mu), axis=-1, keepdims=True)
    y = (u - mu) * lax.rsqrt(var + LN_EPS) * lng_ref[...] + lnb_ref[...]
    y = y * jax.nn.sigmoid(y)
    o_ref[0] = x_ref[0] + _dot(y.astype(BF16), w2_ref[...]) + b2_ref[...]


def _conv_out(u3, x3, w_dw, b_dw, ln_g, ln_b, w_pw2, b_pw2, *, ts=256):
    b, s, d = u3.shape
    tile = pl.BlockSpec((1, ts, d), lambda bi, i: (bi, i, 0))
    halo = pl.BlockSpec((1, HALO, d), lambda bi, i: (bi, jnp.maximum(i * (ts // HALO) - 1, 0), 0))
    return pl.pallas_call(
        functools.partial(_conv_out_kernel, ts=ts),
        out_shape=jax.ShapeDtypeStruct((b, s, d), F32),
        grid=(b, s // ts),
        in_specs=[tile, halo, tile, _resident((CONV_KERNEL, d)), _resident((1, d)), _resident((1, d)),
                  _resident((1, d)), _resident((d, d)), _resident((1, d))],
        out_specs=tile,
        scratch_shapes=[pltpu.VMEM((SUBLANES, ts + HALO, d), F32), pltpu.VMEM((ts, d), F32)],
        compiler_params=_params(2),
        name="conv_out",
    )(u3, u3, x3, w_dw, b_dw, ln_g, ln_b, w_pw2, b_pw2)


def _nsa_in_kernel(x_ref, g_ref, wq_ref, wkv_ref, wgt_ref,
                   qt_ref, kc_ref, vc_ref, ksa_ref, kwa_ref, vst_ref, vwt_ref, gt_ref, *, tm):
    hn = _rms(x_ref[0], g_ref[...]).astype(BF16)
    q = _dot(hn, wq_ref[...]) * (HEAD_DIM ** -0.5 * LOG2E)
    heads_per_chunk = LANES // HEAD_DIM

    def transposed_heads(cols, chunk):
        t = cols[:, chunk * LANES:(chunk + 1) * LANES].T
        return [t[hh * HEAD_DIM:(hh + 1) * HEAD_DIM] for hh in range(heads_per_chunk)]

    for chunk in range(N_HEADS // heads_per_chunk):
        for hh, qt in enumerate(transposed_heads(q, chunk)):
            g, r = divmod(chunk * heads_per_chunk + hh, HEADS_PER_GROUP)
            qt_ref[0, g, r] = qt.astype(BF16)

    gates = jax.nn.sigmoid(_dot(hn, wgt_ref[...]))
    for g in range(N_GROUPS):
        gt_ref[0, g] = gates[:, g * LANES:(g + 1) * LANES].T[:GATE_ROWS]

    kv = _dot(hn, wkv_ref[...])
    pos = pl.program_id(1) * tm + lax.broadcasted_iota(jnp.int32, (tm, SEL_BLOCK), 0)
    blk = lax.broadcasted_iota(jnp.int32, (tm, SEL_BLOCK), 1)
    onehot = jnp.where(pos // SEL_BLOCK == blk, 1.0, 0.0).astype(BF16)
    nothing = jnp.zeros((tm, SEL_BLOCK), BF16)
    for g in range(N_GROUPS):
        def part(j):
            lo = j * KV_DIM + g * HEAD_DIM
            return kv[:, lo:lo + HEAD_DIM]
        kc_ref[0, g] = part(0)
        vc_ref[0, g] = part(1)
        ksa_ref[0, g] = jnp.concatenate([part(2).astype(BF16), onehot], axis=1)
        kwa_ref[0, g] = jnp.concatenate([part(4).astype(BF16), nothing], axis=1)
    v_sel = kv[:, 3 * KV_DIM:4 * KV_DIM]
    v_win = kv[:, 5 * KV_DIM:6 * KV_DIM]
    for chunk in range(N_GROUPS // heads_per_chunk):
        for hh, (vs, vw) in enumerate(zip(transposed_heads(v_sel, chunk), transposed_heads(v_win, chunk))):
            g = chunk * heads_per_chunk + hh
            for j in range(tm // KT):
                vst_ref[0, g, j, 0:HEAD_DIM, :] = vs[:, j * KT:(j + 1) * KT].astype(BF16)
                vst_ref[0, g, j, HEAD_DIM:, :] = jnp.ones((V_ROWS - HEAD_DIM, KT), BF16)
            for j in range(tm // LANES):
                vwt_ref[0, g, j, 0:HEAD_DIM, :] = vw[:, j * LANES:(j + 1) * LANES].astype(BF16)
                vwt_ref[0, g, j, HEAD_DIM:, :] = jnp.ones((V_ROWS - HEAD_DIM, LANES), BF16)


def _nsa_in(x3, norm_g, wq, wkv, wgt, *, tm=512):
    b, s, d = x3.shape
    per_group = lambda width: pl.BlockSpec((1, N_GROUPS, tm, width), lambda bi, i: (bi, 0, i, 0))
    kv_shape = lambda width, dt: jax.ShapeDtypeStruct((b, N_GROUPS, s, width), dt)
    return pl.pallas_call(
        functools.partial(_nsa_in_kernel, tm=tm),
        out_shape=(jax.ShapeDtypeStruct((b, N_GROUPS, HEADS_PER_GROUP, HEAD_DIM, s), BF16),
                   kv_shape(HEAD_DIM, F32), kv_shape(HEAD_DIM, F32),
                   kv_shape(2 * HEAD_DIM, BF16), kv_shape(2 * HEAD_DIM, BF16),
                   jax.ShapeDtypeStruct((b, N_GROUPS, s // KT, V_ROWS, KT), BF16),
                   jax.ShapeDtypeStruct((b, N_GROUPS, s // LANES, V_ROWS, LANES), BF16),
                   jax.ShapeDtypeStruct((b, N_GROUPS, GATE_ROWS, s), F32)),
        grid=(b, s // tm),
        in_specs=[pl.BlockSpec((1, tm, d), lambda bi, i: (bi, i, 0)), _resident((1, d)),
                  _resident(wq.shape), _resident(wkv.shape), _resident(wgt.shape)],
        out_specs=(pl.BlockSpec((1, N_GROUPS, HEADS_PER_GROUP, HEAD_DIM, tm), lambda bi, i: (bi, 0, 0, 0, i)),
                   per_group(HEAD_DIM), per_group(HEAD_DIM), per_group(2 * HEAD_DIM), per_group(2 * HEAD_DIM),
                   pl.BlockSpec((1, N_GROUPS, tm // KT, V_ROWS, KT), lambda bi, i: (bi, 0, i, 0, 0)),
                   pl.BlockSpec((1, N_GROUPS, tm // LANES, V_ROWS, LANES), lambda bi, i: (bi, 0, i, 0, 0)),
                   pl.BlockSpec((1, N_GROUPS, GATE_ROWS, tm), lambda bi, i: (bi, 0, 0, i))),
        compiler_params=_params(2),
        name="nsa_in",
    )(x3, norm_g, wq, wkv, wgt)


def _compress_kernel(kx_ref, vx_ref, pk_ref, pv_ref, wk1_ref, wk2_ref, wv1_ref, wv2_ref, kc_ref, vct_ref, *, nc):
    def mlp(x_ref, p_ref, w1_ref, w2_ref):
        first = second = None
        for l in range(CMP_STRIDE):
            rows = x_ref[0, 0, pl.ds(l, nc, stride=CMP_STRIDE), :]
            lo = _dot((rows + p_ref[l:l + 1, :]).astype(BF16), w1_ref[l])
            hi = _dot((rows + p_ref[CMP_STRIDE + l:CMP_STRIDE + l + 1, :]).astype(BF16), w1_ref[CMP_STRIDE + l])
            first = lo if first is None else first + lo
            second = hi if second is None else second + hi
        hidden = first + pltpu.roll(second, nc - 1, 0)
        return _dot(jax.nn.gelu(hidden).astype(BF16), w2_ref[...])
    kc_ref[0, 0] = mlp(kx_ref, pk_ref, wk1_ref, wk2_ref)[:, :HEAD_DIM].astype(BF16)
    vct_ref[0, 0] = mlp(vx_ref, pv_ref, wv1_ref, wv2_ref).T[:HEAD_DIM].astype(BF16)


def _compress(kx, vx, pos_k, pos_v, wk1, wk2, wv1, wv2):
    b, g, s, width = kx.shape
    nc = s // CMP_STRIDE
    blk = pl.BlockSpec((1, 1, s, width), lambda bi, gi: (bi, gi, 0, 0))
    return pl.pallas_call(
        functools.partial(_compress_kernel, nc=nc),
        out_shape=(jax.ShapeDtypeStruct((b, g, nc, HEAD_DIM), BF16),
                   jax.ShapeDtypeStruct((b, g, HEAD_DIM, nc), BF16)),
        grid=(b, g),
        in_specs=[blk, blk, _resident(pos_k.shape), _resident(pos_v.shape), _resident(wk1.shape),
                  _resident(wk2.shape), _resident(wv1.shape), _resident(wv2.shape)],
        out_specs=(pl.BlockSpec((1, 1, nc, HEAD_DIM), lambda bi, gi: (bi, gi, 0, 0)),
                   pl.BlockSpec((1, 1, HEAD_DIM, nc), lambda bi, gi: (bi, gi, 0, 0))),
        compiler_params=_params(2),
        name="compress",
    )(kx, vx, pos_k, pos_v, wk1, wk2, wv1, wv2)


def _t5_bucket(dist):
    n = jnp.maximum(dist, 0)
    nf = jnp.maximum(n, 1).astype(F32)
    large = MAX_EXACT + (jnp.log(nf / MAX_EXACT) / math.log(MAX_DISTANCE / MAX_EXACT)
                         * (N_BUCKETS - MAX_EXACT)).astype(jnp.int32)
    large = jnp.minimum(large, N_BUCKETS - 1)
    return jnp.where(n < MAX_EXACT, n, large)


def _bias_table_kernel(rb_ref, bucket_ref, o_ref):
    h = pl.program_id(0)
    bucket = bucket_ref[...]
    out = jnp.full(bucket.shape, MASKED, F32)
    for bkt in range(N_BUCKETS):
        out = jnp.where(bucket == bkt, rb_ref[bkt, h] * LOG2E, out)
    o_ref[0] = out


def _bias_table(rel_bias, bucket, *, tr):
    rows, lanes = bucket.shape
    return pl.pallas_call(
        _bias_table_kernel,
        out_shape=jax.ShapeDtypeStruct((N_HEADS, rows, lanes), F32),
        grid=(N_HEADS, rows // tr),
        in_specs=[pl.BlockSpec(memory_space=pltpu.SMEM), pl.BlockSpec((tr, lanes), lambda h, i: (i, 0))],
        out_specs=pl.BlockSpec((1, tr, lanes), lambda h, i: (h, i, 0)),
        compiler_params=_params(2),
        name="bias_table",
    )(rel_bias, bucket)


def _bias_tables(rel_bias, s):
    nc = s // CMP_STRIDE
    key = jnp.arange(LANES)[:, None]
    qry = jnp.arange(LANES)[None, :]

    def toeplitz_blocks(offsets, max_dist):
        dist = LANES * offsets[:, None, None] + qry[None] - key[None]
        valid = (dist >= 0) & (dist < max_dist)
        return jnp.where(valid, _t5_bucket(dist), -1).astype(jnp.int32).reshape(-1, LANES)

    sel_bucket = toeplitz_blocks(jnp.arange(-1, N_FAR + 1), 1 << 30)
    win_bucket = toeplitz_blocks(jnp.arange(-1, N_WIN_BLOCKS), WINDOW)
    cmp_dist = jnp.arange(s)[None, :] - (jnp.arange(nc)[:, None] * CMP_STRIDE + CMP_LEN - 1)
    cmp_bucket = jnp.where(cmp_dist >= 0, _t5_bucket(cmp_dist), -1).astype(jnp.int32)
    t_sel = _bias_table(rel_bias, sel_bucket, tr=sel_bucket.shape[0]).reshape(N_HEADS, N_FAR + 2, LANES, LANES)
    t_win = _bias_table(rel_bias, win_bucket, tr=win_bucket.shape[0]).reshape(N_HEADS, N_WIN_BLOCKS + 1, LANES, LANES)
    t_cmp = _bias_table(rel_bias, cmp_bucket, tr=nc)
    return t_sel, t_win, t_cmp


def _sublane_fold(x, op):
    parts = [x[k * SUBLANES:(k + 1) * SUBLANES] for k in range(x.shape[0] // SUBLANES)]
    while len(parts) > 1:
        parts = [op(parts[k], parts[k + 1]) for k in range(0, len(parts) - 1, 2)] + parts[len(parts) & ~1:]
    return parts[0]


def _cmp_kernel(qt_ref, kc_ref, vct_ref, tc_ref, ovt_ref, *rest, n_blk, k_sel, first_tile):
    oct_ref, mtt_ref = rest[-2:]
    heads = range(HEADS_PER_GROUP)
    q_t = jnp.concatenate([qt_ref[0, 0, r] for r in heads], axis=1)
    bias = jnp.concatenate([tc_ref[r] for r in heads], axis=1)
    s = _dot(kc_ref[0, 0], q_t) + bias
    m = jnp.maximum(jnp.max(s, axis=0, keepdims=True), M_INIT)
    e = jnp.exp2(s - m)
    pr = e * (1.0 / jnp.maximum(jnp.sum(e, axis=0, keepdims=True), 1e-30))
    o_t = _dot(vct_ref[0, 0], pr.astype(BF16))
    for r in heads:
        oct_ref[0, 0, r] = o_t[:, r * CQ:(r + 1) * CQ]

    psum = pr[:, 0:CQ]
    for r in heads[1:]:
        psum = psum + pr[:, r * CQ:(r + 1) * CQ]
    hi = psum.astype(BF16)
    rem = psum - hi.astype(F32)
    mid = rem.astype(BF16)
    lo = (rem - mid.astype(F32)).astype(BF16)
    ov_t = ovt_ref[...]
    imp = _dot(ov_t, hi) + _dot(ov_t, mid) + _dot(ov_t, lo)

    t = (first_tile + pl.program_id(0)) * CQ + lax.broadcasted_iota(jnp.int32, (LANES, CQ), 1)
    j = lax.broadcasted_iota(jnp.int32, (LANES, CQ), 0)
    cur = t // SEL_BLOCK
    forced = (j == 0) | (j == cur) | (j == cur - 1)
    score = jnp.where(forced, FORCE_SCORE, jnp.where(j * SEL_BLOCK <= t, imp, -1.0))

    terms = []
    for v in range(n_blk // SUBLANES):
        mine = score[v * SUBLANES:(v + 1) * SUBLANES]
        jj = v * SUBLANES + lax.broadcasted_iota(jnp.int32, (SUBLANES, CQ), 0)
        rank = jnp.zeros((SUBLANES, CQ), jnp.int32)
        for i in range(n_blk):
            other = score[i:i + 1]
            if i < v * SUBLANES:
                beats = other >= mine
            elif i >= (v + 1) * SUBLANES:
                beats = other > mine
            else:
                beats = (other > mine) | ((other == mine) & (i < jj))
            rank = rank + beats.astype(jnp.int32)
        terms.append(jnp.where(rank < k_sel, 0.0, MASK_TERM))
    if n_blk < SEL_BLOCK:
        terms.append(jnp.full((SEL_BLOCK - n_blk, CQ), MASK_TERM, F32))
    mtt_ref[0, 0] = jnp.concatenate(terms, axis=0).astype(BF16)


def _cmp_branch(q_t, kc, vc_t, t_cmp, overlap_t):
    b, _, _, _, s = q_t.shape
    nc = kc.shape[2]
    n_sel = s // SEL_BLOCK
    tiles_per_part = s // CQ // CMP_PARTS
    out_shape = (jax.ShapeDtypeStruct((b, N_GROUPS, HEADS_PER_GROUP, HEAD_DIM, s), F32),
                 jax.ShapeDtypeStruct((b, N_GROUPS, SEL_BLOCK, s), BF16))
    outs = ()
    for part in range(CMP_PARTS):
        first = part * tiles_per_part
        n_blk = n_sel * (part + 1) // CMP_PARTS
        nc_part = min(nc, -(-(nc * (part + 1) // CMP_PARTS) // LANES) * LANES)
        qspec = pl.BlockSpec((1, 1, HEADS_PER_GROUP, HEAD_DIM, CQ),
                             lambda qi, gi, bi, first=first: (bi, gi, 0, 0, first + qi))
        outs = pl.pallas_call(
            functools.partial(_cmp_kernel, n_blk=n_blk, k_sel=min(N_SELECT, n_sel), first_tile=first),
            out_shape=out_shape,
            grid=(tiles_per_part, N_GROUPS, b),
            in_specs=[qspec,
                      pl.BlockSpec((1, 1, nc_part, HEAD_DIM), lambda qi, gi, bi: (bi, gi, 0, 0)),
                      pl.BlockSpec((1, 1, HEAD_DIM, nc_part), lambda qi, gi, bi: (bi, gi, 0, 0)),
                      pl.BlockSpec((HEADS_PER_GROUP, nc_part, CQ),
                                   lambda qi, gi, bi, first=first: (gi, 0, first + qi)),
                      _resident((LANES, nc_part))] + [pl.BlockSpec(memory_space=pl.ANY)] * len(outs),
            out_specs=(qspec, pl.BlockSpec((1, 1, SEL_BLOCK, CQ),
                                           lambda qi, gi, bi, first=first: (bi, gi, 0, first + qi))),
            input_output_aliases={5 + k: k for k in range(len(outs))},
            compiler_params=_params(3),
            name="cmp_branch",
        )(q_t, kc, vc_t, t_cmp, overlap_t[:, :nc_part], *outs)
    return outs


def _key_tiles(tq):
    return (tq * QB + QB + KB - 1) // KB


def _selwin_kernel(qta_ref, qtb_ref, mta_ref, mtb_ref, ksa_ref, kwa_ref, vst_ref, vwt_ref, tsel_ref, twin_ref,
                   oca_ref, ocb_ref, gta_ref, gtb_ref, oa_ref, ob_ref,
                   qa_scr, part_scr, win_scr, wsc_scr, ssc_scr, *, n_q):
    i = pl.program_id(2)
    heads = range(HEADS_PER_GROUP)
    cols = HEADS_PER_GROUP * QT
    n_slots = part_scr.shape[0]
    q_tiles = (i, n_q - 1 - i)
    n_first = _key_tiles(i)

    for w, (qt_ref, mt_ref) in enumerate(((qta_ref, mta_ref), (qtb_ref, mtb_ref))):
        qa_scr[w] = jnp.concatenate(
            [jnp.concatenate([qt_ref[0, 0, r], mt_ref[0, 0]], axis=0) for r in heads], axis=1)

    def slot_info(slot):
        second = slot >= n_first
        w = second.astype(jnp.int32)
        kt = jnp.where(second, slot - n_first, slot)
        tq = jnp.where(second, q_tiles[1], q_tiles[0])
        return second, w, kt, tq

    def normalised(aug):
        return aug[:HEAD_DIM] * (1.0 / jnp.maximum(aug[HEAD_DIM:HEAD_DIM + 1], 1e-30))

    def win_first_block(w):
        return jnp.maximum(q_tiles[w] * QB - WINDOW // LANES, 0)

    def biased_scores(raw, table_ref, block_index, out_ref):
        folded = []
        for r in heads:
            for qb in range(QB):
                cs = slice(r * QT + qb * LANES, r * QT + (qb + 1) * LANES)
                fold = None
                for mb in range(raw.shape[0] // LANES):
                    rs = slice(mb * LANES, (mb + 1) * LANES)
                    sb = raw[rs, cs] + table_ref[r, block_index(mb, qb)]
                    out_ref[rs, cs] = sb
                    part = _sublane_fold(sb, jnp.maximum)
                    fold = part if fold is None else jnp.maximum(fold, part)
                folded.append(fold)
        return jnp.maximum(jnp.max(jnp.concatenate(folded, axis=1), axis=0, keepdims=True), M_INIT)

    def win_scores(w):
        first_blk = win_first_block(w)
        w0 = pl.multiple_of(first_blk * LANES, LANES)
        raw = _dot(kwa_ref[0, 0, pl.ds(w0, WIN_TILE_BLOCKS * LANES), :], qa_scr[w])

        def block_index(mb, qb):
            a = q_tiles[w] * QB + qb - (first_blk + mb)
            return jnp.where((a < 0) | (a >= N_WIN_BLOCKS), 0, a + 1)
        return biased_scores(raw, twin_ref, block_index, wsc_scr.at[w])

    def win_finish(w, m_w):
        first_blk = win_first_block(w)
        win = None
        for mb in range(WIN_TILE_BLOCKS):
            e_w = jnp.exp2(wsc_scr[w, mb * LANES:(mb + 1) * LANES, :] - m_w).astype(BF16)
            term = _dot(vwt_ref[0, 0, first_blk + mb], e_w)
            win = term if win is None else win + term
        win_scr[w] = normalised(win)

    m_slots = {}
    ring = ssc_scr.shape[0]

    def sel_scores(slot):
        second, w, kt, tq = slot_info(slot)
        k0 = pl.multiple_of(kt * KT, KT)
        ssc_scr[slot % ring] = _dot(ksa_ref[0, 0, pl.ds(k0, KT), :], qa_scr[w])

    def sel_finish(slot, _):
        second, w, kt, tq = slot_info(slot)
        block_index = lambda mb, qb: jnp.clip(tq * QB + qb - (kt * KB + mb), -1, N_FAR) + 1
        buf = ssc_scr.at[slot % ring]
        m_slot = m_slots[slot] = biased_scores(buf, tsel_ref, block_index, buf)
        p = jnp.exp2(ssc_scr[slot % ring] - m_slot).astype(BF16)
        part_scr[slot] = _dot(vst_ref[0, 0, kt], p)[:PART_ROWS]

    items = [(sel_scores, sel_finish, k) for k in range(n_slots)] + [(win_scores, win_finish, w) for w in range(2)]

    def merge_slots():
        max_first = _key_tiles(n_q // 2 - 1)
        seconds = [slot_info(slot)[0] for slot in range(n_slots)]
        m_fin = [functools.reduce(jnp.maximum, [jnp.where(seconds[k], M_INIT, m_slots[k]) for k in range(max_first)]),
                 functools.reduce(jnp.maximum, [jnp.where(seconds[k], m_slots[k], M_INIT) if k < max_first
                                                else m_slots[k] for k in range(1, n_slots)])]
        merged = []
        for w in range(2):
            acc = None
            for slot in (range(max_first) if w == 0 else range(1, n_slots)):
                weight = jnp.exp2(m_slots[slot] - m_fin[w])
                if 0 < slot < max_first:
                    owned = seconds[slot] if w == 1 else jnp.logical_not(seconds[slot])
                    weight = jnp.where(owned, weight, 0.0)
                term = part_scr[slot] * weight
                acc = term if acc is None else acc + term
            merged.append(normalised(acc))
        return merged

    pending = {}
    o_sels = None
    for n in range(len(items) + SCORE_LOOKAHEAD):
        if n < len(items):
            scores, _, arg = items[n]
            pending[n] = scores(arg)
        if n >= SCORE_LOOKAHEAD:
            _, finish, arg = items[n - SCORE_LOOKAHEAD]
            finish(arg, pending.pop(n - SCORE_LOOKAHEAD))
            if n - SCORE_LOOKAHEAD == n_slots - 1:
                o_sels = merge_slots()

    for w, (oc_ref, gt_ref, o_ref) in enumerate(((oca_ref, gta_ref, oa_ref), (ocb_ref, gtb_ref, ob_ref))):
        o_sel = o_sels[w]
        o_win = win_scr[w]

        gates = gt_ref[0, 0]
        mixed = []
        for r in heads:
            cs = slice(r * QT, (r + 1) * QT)
            g_c, g_s, g_w = (gates[3 * r + j:3 * r + j + 1] for j in range(3))
            mixed.append(g_c * oc_ref[0, 0, r] + g_s * o_sel[:, cs] + g_w * o_win[:, cs])
        pairs = [jnp.concatenate(mixed[k:k + 2], axis=0).T for k in range(0, HEADS_PER_GROUP, 2)]
        o_ref[0] = jnp.concatenate(pairs, axis=1).astype(BF16)


def _selwin_branch(q_t, mterm_t, ksa, kwa, vs_t, vw_t, t_sel, t_win, o_cmp_t, gates_t):
    b, _, _, _, s = q_t.shape
    n_q = s // QT
    half = n_q // 2
    n_slots = _key_tiles(0) + _key_tiles(n_q - 1)
    assert all(_key_tiles(i) + _key_tiles(n_q - 1 - i) == n_slots for i in range(half))
    cols = HEADS_PER_GROUP * QT
    fwd = lambda gi, bi, i: i
    bwd = lambda gi, bi, i: n_q - 1 - i
    qspec = lambda pick: pl.BlockSpec((1, 1, HEADS_PER_GROUP, HEAD_DIM, QT),
                                      lambda gi, bi, i: (bi, gi, 0, 0, pick(gi, bi, i)))
    rowspec = lambda rows, pick: pl.BlockSpec((1, 1, rows, QT), lambda gi, bi, i: (bi, gi, 0, pick(gi, bi, i)))
    whole = lambda arr: pl.BlockSpec((1, 1) + arr.shape[2:], lambda gi, bi, i: (bi, gi) + (0,) * (arr.ndim - 2))
    tab = lambda n: pl.BlockSpec((HEADS_PER_GROUP, n, LANES, LANES), lambda gi, bi, i: (gi, 0, 0, 0))
    out_shape = jax.ShapeDtypeStruct((b, s // 2, N_HEADS * HEAD_DIM), BF16)
    return pl.pallas_call(
        functools.partial(_selwin_kernel, n_q=n_q),
        out_shape=(out_shape, out_shape),
        grid=(N_GROUPS, b, half),
        in_specs=[qspec(fwd), qspec(bwd), rowspec(SEL_BLOCK, fwd), rowspec(SEL_BLOCK, bwd),
                  whole(ksa), whole(kwa), whole(vs_t), whole(vw_t),
                  tab(t_sel.shape[1]), tab(t_win.shape[1]),
                  qspec(fwd), qspec(bwd), rowspec(GATE_ROWS, fwd), rowspec(GATE_ROWS, bwd)],
        out_specs=(pl.BlockSpec((1, QT, GROUP_DIM), lambda gi, bi, i: (bi, i, gi)),
                   pl.BlockSpec((1, QT, GROUP_DIM), lambda gi, bi, i: (bi, half - 1 - i, gi))),
        scratch_shapes=[pltpu.VMEM((2, 2 * HEAD_DIM, cols), BF16),
                        pltpu.VMEM((n_slots, PART_ROWS, cols), F32),
                        pltpu.VMEM((2, HEAD_DIM, cols), F32),
                        pltpu.VMEM((2, WIN_TILE_BLOCKS * LANES, cols), F32),
                        pltpu.VMEM((SCORE_LOOKAHEAD + 1, KT, cols), F32)],
        compiler_params=_params(3),
        name="selwin_branch",
    )(q_t, q_t, mterm_t, mterm_t, ksa, kwa, vs_t, vw_t, t_sel, t_win, o_cmp_t, o_cmp_t, gates_t, gates_t)


def _nsa_mixer(x3, norm_g, w_in, pos_k, pos_v, wk1, wk2, wv1, wv2, tables):
    b, s, d = x3.shape
    n_q = s // QT
    assert n_q % 2 == 0 and s % CQ == 0 and s >= WIN_TILE_BLOCKS * LANES and s // SEL_BLOCK <= SEL_BLOCK
    t_sel, t_win, t_cmp = tables
    q_dim = N_HEADS * HEAD_DIM
    wq = w_in[:, :q_dim].astype(BF16)
    wkv = w_in[:, q_dim:q_dim + 6 * KV_DIM].astype(BF16)
    wg = w_in[:, q_dim + 6 * KV_DIM:].reshape(d, N_GROUPS, 3 * HEADS_PER_GROUP)
    wg = jnp.pad(wg, ((0, 0), (0, 0), (0, LANES - 3 * HEADS_PER_GROUP))).reshape(d, N_GROUPS * LANES).astype(BF16)
    q_t, kc_raw, vc_raw, ksa, kwa, vs_t, vw_t, gates_t = _nsa_in(x3, norm_g, wq, wkv, wg)

    nc = s // CMP_STRIDE
    per_row = lambda w: w.reshape(CMP_LEN, HEAD_DIM, -1).astype(BF16)
    widen = lambda w: jnp.pad(w, ((0, 0), (0, LANES - HEAD_DIM))).astype(BF16)
    kc, vc_t = _compress(kc_raw, vc_raw, pos_k, pos_v, per_row(wk1), widen(wk2), per_row(wv1), widen(wv2))

    n_cmp = (s - CMP_LEN) // CMP_STRIDE + 1
    c_start = jnp.arange(nc)[None, :] * CMP_STRIDE
    n_start = jnp.arange(LANES)[:, None] * SEL_BLOCK
    overlap_t = ((c_start < n_start + SEL_BLOCK) & (c_start + CMP_LEN > n_start)
                 & (jnp.arange(nc)[None, :] < n_cmp) & (jnp.arange(LANES)[:, None] < s // SEL_BLOCK))
    o_cmp_t, mterm_t = _cmp_branch(q_t, kc, vc_t, t_cmp, overlap_t.astype(BF16))
    return _selwin_branch(q_t, mterm_t, ksa, kwa, vs_t, vw_t, t_sel, t_win, o_cmp_t, gates_t)


def kernel(x, p, rel_bias, ffn1_norm, ffn1_w_gate, ffn1_w_up, ffn1_w_down, mix_norm, ffn2_norm, ffn2_w_gate, ffn2_w_up, ffn2_w_down, ple_norm, ple_w_gate, ple_w_in, conv_w_pw1, conv_b_pw1, conv_w_dw, conv_b_dw, conv_ln_g, conv_ln_b, conv_w_pw2, conv_b_pw2, nsa_w_in, nsa_cmp_pos_k, nsa_cmp_pos_v, nsa_cmp_wk1, nsa_cmp_wk2, nsa_cmp_wv1, nsa_cmp_wv2, nsa_w_out, final_norm):
    b, s, d = x.shape
    depth = ffn1_norm.shape[0]
    row = lambda v: v.reshape(1, -1)
    tables = _bias_tables(rel_bias, s)
    bf = lambda w: w.astype(BF16)
    for i in range(depth):
        j = i // 2
        ffn1 = (row(ffn1_norm[i]), bf(ffn1_w_gate[i]), bf(ffn1_w_up[i]), bf(ffn1_w_down[i]))
        ffn2 = (row(ffn2_norm[i]), bf(ffn2_w_gate[i]), bf(ffn2_w_up[i]), bf(ffn2_w_down[i]))
        ple = ((p, i), row(ple_norm[i]), bf(ple_w_gate[i]), bf(ple_w_in[i]))
        if i == depth - 1:
            ple += (row(final_norm),)
        if i % 2 == 0:
            x, u = _ffn(x, *ffn1, glu=(row(mix_norm[i]), bf(conv_w_pw1[j]), row(conv_b_pw1[j])))
            x = _conv_out(u, x, conv_w_dw[j], row(conv_b_dw[j]), row(conv_ln_g[j]), row(conv_ln_b[j]),
                          bf(conv_w_pw2[j]), row(conv_b_pw2[j]))
            x = _ffn(x, *ffn2, ple=ple)
        else:
            x = _ffn(x, *ffn1)
            o_lo, o_hi = _nsa_mixer(x, row(mix_norm[i]), nsa_w_in[j], nsa_cmp_pos_k[j], nsa_cmp_pos_v[j],
                                    nsa_cmp_wk1[j], nsa_cmp_wk2[j], nsa_cmp_wv1[j], nsa_cmp_wv2[j], tables)
            x = _ffn(x, *ffn2, proj=(o_lo, o_hi, bf(nsa_w_out[j])), ple=ple)
    return x
```

```python
import functools
import math

import jax
import jax.numpy as jnp
from jax import lax
from jax.experimental import pallas as pl
from jax.experimental.pallas import tpu as pltpu

F32 = jnp.float32
BF16 = jnp.bfloat16

D_MODEL = 1024
D_FF = 2816
CONV_KERNEL = 31
HEAD_DIM = 64
N_HEADS = 16
N_GROUPS = 4
HEADS_PER_GROUP = 4
GROUP_DIM = HEADS_PER_GROUP * HEAD_DIM
KV_DIM = N_GROUPS * HEAD_DIM
CMP_LEN = 32
CMP_STRIDE = 16
SEL_BLOCK = 64
N_SELECT = 16
WINDOW = 512
N_BUCKETS = 32
MAX_EXACT = 16
MAX_DISTANCE = 2048
RMS_EPS = 1e-6
LN_EPS = 1e-5
FORCE_SCORE = 1e9

LANES = 128
SUBLANES = 8
QT = 128
KT = 256
QB = QT // LANES
KB = KT // LANES
CQ = 512
CMP_PARTS = 4
ROWS_PER_DOT = 4
GATE_ROWS = 16
SCORE_LOOKAHEAD = 2
HALO = 32
MASKED = -1e30
MASK_TERM = -(2.0 ** 100)
M_INIT = -1e20
LOG2E = math.log2(math.e)
V_ROWS = HEAD_DIM + 16
PART_ROWS = HEAD_DIM + SUBLANES
N_FAR = MAX_DISTANCE // LANES + 1
N_CMP_FAR = -(-(MAX_DISTANCE + (SUBLANES - 1) * CMP_STRIDE + CMP_LEN - 1) // LANES)
N_WIN_BLOCKS = WINDOW // LANES + 1
WIN_TILE_BLOCKS = WINDOW // LANES + QB
VMEM_LIMIT = 56 * 1024 * 1024


def _params(n_axes):
    return pltpu.CompilerParams(dimension_semantics=("arbitrary",) * n_axes,
                                vmem_limit_bytes=VMEM_LIMIT)


def _resident(shape):
    nd = len(shape)
    return pl.BlockSpec(shape, lambda *_: (0,) * nd, pipeline_mode=pl.Buffered(1))


def _rms(x, g):
    return x * lax.rsqrt(jnp.mean(x * x, axis=-1, keepdims=True) + RMS_EPS) * g


def _dot(a, b):
    return jnp.dot(a, b, preferred_element_type=F32)


def _ffn_kernel(*refs, row_parts, half_tiles, epilogue):
    refs = list(refs)
    take = lambda n: [refs.pop(0) for _ in range(n)]
    (x_ref,) = take(1)
    proj = take(3) if half_tiles is not None else None
    g_ref, wg_ref, wu_ref, wd_ref = take(4)
    extra = take({"none": 0, "ple": 4, "ple_final": 5, "glu": 3}[epilogue])
    (o_ref,) = take(1)
    u_ref = take(1)[0] if epilogue == "glu" else None

    rows_per_part = x_ref.shape[1] // row_parts
    parts = [slice(k * rows_per_part, (k + 1) * rows_per_part) for k in range(row_parts)]
    hidden = []
    for rows in parts:
        x = x_ref[0, rows, :]
        if proj is not None:
            lo_ref, hi_ref, wo_ref = proj
            o = jnp.where(pl.program_id(1) < half_tiles, lo_ref[0, rows, :], hi_ref[0, rows, :])
            x = x + _dot(o, wo_ref[...])
            o_ref[0, rows, :] = x
        xn = _rms(x, g_ref[...]).astype(BF16)
        gate = _dot(xn, wg_ref[...])
        up = _dot(xn, wu_ref[...])
        hidden.append(((gate * jax.nn.sigmoid(gate)) * up).astype(BF16))
    for rows, h in zip(parts, hidden):
        x = x_ref[0, rows, :] if proj is None else o_ref[0, rows, :]
        y = x + 0.5 * _dot(h, wd_ref[...])
        if epilogue in ("ple", "ple_final"):
            p_ref, pg_ref, pwg_ref, pwi_ref = extra[:4]
            gate = jax.nn.sigmoid(_dot(_rms(y, pg_ref[...]).astype(BF16), pwg_ref[...]))
            y = y + gate * _dot(p_ref[0, rows, :].astype(BF16), pwi_ref[...])
            if epilogue == "ple_final":
                y = _rms(y, extra[4][...])
        elif epilogue == "glu":
            cg_ref, cw_ref, cb_ref = extra
            d = y.shape[-1]
            hn = _rms(y, cg_ref[...]).astype(BF16)
            a = _dot(hn, cw_ref[:, :d]) + cb_ref[:, :d]
            gt = _dot(hn, cw_ref[:, d:]) + cb_ref[:, d:]
            u_ref[0, rows, :] = a * jax.nn.sigmoid(gt)
        o_ref[0, rows, :] = y


def _ffn(x3, norm_g, wg, wu, wd, *, proj=None, ple=None, glu=None):
    b, s, d = x3.shape
    tm, row_parts = (512, 2) if proj is not None else (1024, 4)
    tile = pl.BlockSpec((1, tm, d), lambda bi, i: (bi, i, 0))
    operands, in_specs = [x3], [tile]
    half_tiles = None
    if proj is not None:
        o_lo, o_hi, w_out = proj
        half_tiles = s // 2 // tm
        width = o_lo.shape[2]
        operands += [o_lo, o_hi, w_out]
        in_specs += [pl.BlockSpec((1, tm, width), lambda bi, i: (bi, jnp.minimum(i, half_tiles - 1), 0)),
                     pl.BlockSpec((1, tm, width), lambda bi, i: (bi, jnp.maximum(i - half_tiles, 0), 0)),
                     _resident(w_out.shape)]
    weights = [norm_g, wg, wu, wd]
    epilogue = "none"
    out_shape = [jax.ShapeDtypeStruct((b, s, d), F32)]
    if ple is not None:
        epilogue = "ple" if len(ple) == 4 else "ple_final"
        p_all, layer = ple[0]
        operands_after = [p_all]
        specs_after = [pl.BlockSpec((None, 1, tm, p_all.shape[3]), lambda bi, i: (layer, bi, i, 0))]
        weights_after = list(ple[1:])
    elif glu is not None:
        epilogue = "glu"
        operands_after, specs_after, weights_after = [], [], list(glu)
        out_shape.append(jax.ShapeDtypeStruct((b, s, d), F32))
    else:
        operands_after, specs_after, weights_after = [], [], []
    operands += weights + operands_after + weights_after
    in_specs += [_resident(w.shape) for w in weights] + specs_after + [_resident(w.shape) for w in weights_after]
    out = pl.pallas_call(
        functools.partial(_ffn_kernel, row_parts=row_parts, half_tiles=half_tiles, epilogue=epilogue),
        out_shape=tuple(out_shape),
        grid=(b, s // tm),
        in_specs=in_specs,
        out_specs=tuple([tile] * len(out_shape)),
        compiler_params=_params(2),
        name="ffn",
    )(*operands)
    return out if glu is not None else out[0]


def _conv_out_kernel(u_ref, halo_ref, x_ref, wdw_ref, bdw_ref, lng_ref, lnb_ref, w2_ref, b2_ref,
                     o_ref, pad_ref, conv_ref, *, ts):
    first = pl.program_id(1) == 0
    pad_ref[0, 0:HALO, :] = jnp.where(first, 0.0, halo_ref[0])
    pad_ref[0, HALO:, :] = u_ref[0]
    shifted_rows = ts + HALO - SUBLANES
    for ph in range(1, SUBLANES):
        pad_ref[ph, 0:shifted_rows, :] = pad_ref[0, ph:ph + shifted_rows, :]
    off = HALO - (CONV_KERNEL - 1)
    d = u_ref.shape[-1]
    for c in range(d // LANES):
        cs = slice(c * LANES, (c + 1) * LANES)
        acc = jnp.zeros((ts, LANES), F32)
        for k in range(CONV_KERNEL):
            ph = (off + k) % SUBLANES
            base = off + k - ph
            acc = acc + wdw_ref[k:k + 1, cs] * pad_ref[ph, base:base + ts, cs]
        conv_ref[:, cs] = acc + bdw_ref[:, cs]
    u = conv_ref[...]
    mu = jnp.mean(u, axis=-1, keepdims=True)
    var = jnp.mean(jnp.square(u - mu), axis=-1, keepdims=True)
    y = (u - mu) * lax.rsqrt(var + LN_EPS) * lng_ref[...] + lnb_ref[...]
    y = y * jax.nn.sigmoid(y)
    o_ref[0] = x_ref[0] + _dot(y.astype(BF16), w2_ref[...]) + b2_ref[...]


def _conv_out(u3, x3, w_dw, b_dw, ln_g, ln_b, w_pw2, b_pw2, *, ts=256):
    b, s, d = u3.shape
    tile = pl.BlockSpec((1, ts, d), lambda bi, i: (bi, i, 0))
    halo = pl.BlockSpec((1, HALO, d), lambda bi, i: (bi, jnp.maximum(i * (ts // HALO) - 1, 0), 0))
    return pl.pallas_call(
        functools.partial(_conv_out_kernel, ts=ts),
        out_shape=jax.ShapeDtypeStruct((b, s, d), F32),
        grid=(b, s // ts),
        in_specs=[tile, halo, tile, _resident((CONV_KERNEL, d)), _resident((1, d)), _resident((1, d)),
                  _resident((1, d)), _resident((d, d)), _resident((1, d))],
        out_specs=tile,
        scratch_shapes=[pltpu.VMEM((SUBLANES, ts + HALO, d), F32), pltpu.VMEM((ts, d), F32)],
        compiler_params=_params(2),
        name="conv_out",
    )(u3, u3, x3, w_dw, b_dw, ln_g, ln_b, w_pw2, b_pw2)


def _nsa_in_kernel(x_ref, g_ref, wq_ref, wkv_ref, wgt_ref,
                   qt_ref, kc_ref, vc_ref, ksa_ref, kwa_ref, vst_ref, vwt_ref, gt_ref, *, tm):
    hn = _rms(x_ref[0], g_ref[...]).astype(BF16)
    q = _dot(hn, wq_ref[...]) * (HEAD_DIM ** -0.5 * LOG2E)
    heads_per_chunk = LANES // HEAD_DIM

    def transposed_heads(cols, chunk):
        t = cols[:, chunk * LANES:(chunk + 1) * LANES].T
        return [t[hh * HEAD_DIM:(hh + 1) * HEAD_DIM] for hh in range(heads_per_chunk)]

    for chunk in range(N_HEADS // heads_per_chunk):
        for hh, qt in enumerate(transposed_heads(q, chunk)):
            g, r = divmod(chunk * heads_per_chunk + hh, HEADS_PER_GROUP)
            qt_ref[0, g, r] = qt.astype(BF16)

    gates = jax.nn.sigmoid(_dot(hn, wgt_ref[...]))
    for g in range(N_GROUPS):
        gt_ref[0, g] = gates[:, g * LANES:(g + 1) * LANES].T[:GATE_ROWS]

    kv = _dot(hn, wkv_ref[...])
    pos = pl.program_id(1) * tm + lax.broadcasted_iota(jnp.int32, (tm, SEL_BLOCK), 0)
    blk = lax.broadcasted_iota(jnp.int32, (tm, SEL_BLOCK), 1)
    onehot = jnp.where(pos // SEL_BLOCK == blk, 1.0, 0.0).astype(BF16)
    nothing = jnp.zeros((tm, SEL_BLOCK), BF16)
    for g in range(N_GROUPS):
        def part(j):
            lo = j * KV_DIM + g * HEAD_DIM
            return kv[:, lo:lo + HEAD_DIM]
        kc_ref[0, g] = part(0)
        vc_ref[0, g] = part(1)
        ksa_ref[0, g] = jnp.concatenate([part(2).astype(BF16), onehot], axis=1)
        kwa_ref[0, g] = jnp.concatenate([part(4).astype(BF16), nothing], axis=1)
    v_sel = kv[:, 3 * KV_DIM:4 * KV_DIM]
    v_win = kv[:, 5 * KV_DIM:6 * KV_DIM]
    for chunk in range(N_GROUPS // heads_per_chunk):
        for hh, (vs, vw) in enumerate(zip(transposed_heads(v_sel, chunk), transposed_heads(v_win, chunk))):
            g = chunk * heads_per_chunk + hh
            for j in range(tm // KT):
                vst_ref[0, g, j, 0:HEAD_DIM, :] = vs[:, j * KT:(j + 1) * KT].astype(BF16)
                vst_ref[0, g, j, HEAD_DIM:, :] = jnp.ones((V_ROWS - HEAD_DIM, KT), BF16)
            for j in range(tm // LANES):
                vwt_ref[0, g, j, 0:HEAD_DIM, :] = vw[:, j * LANES:(j + 1) * LANES].astype(BF16)
                vwt_ref[0, g, j, HEAD_DIM:, :] = jnp.ones((V_ROWS - HEAD_DIM, LANES), BF16)


def _nsa_in(x3, norm_g, wq, wkv, wgt, *, tm=512):
    b, s, d = x3.shape
    per_group = lambda width: pl.BlockSpec((1, N_GROUPS, tm, width), lambda bi, i: (bi, 0, i, 0))
    kv_shape = lambda width, dt: jax.ShapeDtypeStruct((b, N_GROUPS, s, width), dt)
    return pl.pallas_call(
        functools.partial(_nsa_in_kernel, tm=tm),
        out_shape=(jax.ShapeDtypeStruct((b, N_GROUPS, HEADS_PER_GROUP, HEAD_DIM, s), BF16),
                   kv_shape(HEAD_DIM, F32), kv_shape(HEAD_DIM, F32),
                   kv_shape(2 * HEAD_DIM, BF16), kv_shape(2 * HEAD_DIM, BF16),
                   jax.ShapeDtypeStruct((b, N_GROUPS, s // KT, V_ROWS, KT), BF16),
                   jax.ShapeDtypeStruct((b, N_GROUPS, s // LANES, V_ROWS, LANES), BF16),
                   jax.ShapeDtypeStruct((b, N_GROUPS, GATE_ROWS, s), F32)),
        grid=(b, s // tm),
        in_specs=[pl.BlockSpec((1, tm, d), lambda bi, i: (bi, i, 0)), _resident((1, d)),
                  _resident(wq.shape), _resident(wkv.shape), _resident(wgt.shape)],
        out_specs=(pl.BlockSpec((1, N_GROUPS, HEADS_PER_GROUP, HEAD_DIM, tm), lambda bi, i: (bi, 0, 0, 0, i)),
                   per_group(HEAD_DIM), per_group(HEAD_DIM), per_group(2 * HEAD_DIM), per_group(2 * HEAD_DIM),
                   pl.BlockSpec((1, N_GROUPS, tm // KT, V_ROWS, KT), lambda bi, i: (bi, 0, i, 0, 0)),
                   pl.BlockSpec((1, N_GROUPS, tm // LANES, V_ROWS, LANES), lambda bi, i: (bi, 0, i, 0, 0)),
                   pl.BlockSpec((1, N_GROUPS, GATE_ROWS, tm), lambda bi, i: (bi, 0, 0, i))),
        compiler_params=_params(2),
        name="nsa_in",
    )(x3, norm_g, wq, wkv, wgt)


def _compress_kernel(kx_ref, vx_ref, pk_ref, pv_ref, wk1_ref, wk2_ref, wv1_ref, wv2_ref, kc_ref, vct_ref, *, nc):
    def mlp(x_ref, p_ref, w1_ref, w2_ref):
        first = second = None
        for l0 in range(0, CMP_STRIDE, ROWS_PER_DOT):
            rows = [x_ref[0, 0, pl.ds(l, nc, stride=CMP_STRIDE), :] for l in range(l0, l0 + ROWS_PER_DOT)]

            def shifted(offset):
                return jnp.concatenate([(r + p_ref[offset + l:offset + l + 1, :]).astype(BF16)
                                        for l, r in zip(range(l0, l0 + ROWS_PER_DOT), rows)], axis=1)
            lo = _dot(shifted(0), w1_ref[l0 // ROWS_PER_DOT])
            hi = _dot(shifted(CMP_STRIDE), w1_ref[(CMP_STRIDE + l0) // ROWS_PER_DOT])
            first = lo if first is None else first + lo
            second = hi if second is None else second + hi
        hidden = first + pltpu.roll(second, nc - 1, 0)
        return _dot(jax.nn.gelu(hidden).astype(BF16), w2_ref[...])
    kc_ref[0, 0] = mlp(kx_ref, pk_ref, wk1_ref, wk2_ref)[:, :HEAD_DIM].astype(BF16)
    vct_ref[0, 0] = mlp(vx_ref, pv_ref, wv1_ref, wv2_ref).T[:HEAD_DIM].astype(BF16)


def _compress(kx, vx, pos_k, pos_v, wk1, wk2, wv1, wv2):
    b, g, s, width = kx.shape
    nc = s // CMP_STRIDE
    blk = pl.BlockSpec((1, 1, s, width), lambda bi, gi: (bi, gi, 0, 0))
    return pl.pallas_call(
        functools.partial(_compress_kernel, nc=nc),
        out_shape=(jax.ShapeDtypeStruct((b, g, nc, HEAD_DIM), BF16),
                   jax.ShapeDtypeStruct((b, g, HEAD_DIM, nc), BF16)),
        grid=(b, g),
        in_specs=[blk, blk, _resident(pos_k.shape), _resident(pos_v.shape), _resident(wk1.shape),
                  _resident(wk2.shape), _resident(wv1.shape), _resident(wv2.shape)],
        out_specs=(pl.BlockSpec((1, 1, nc, HEAD_DIM), lambda bi, gi: (bi, gi, 0, 0)),
                   pl.BlockSpec((1, 1, HEAD_DIM, nc), lambda bi, gi: (bi, gi, 0, 0))),
        compiler_params=_params(2),
        name="compress",
    )(kx, vx, pos_k, pos_v, wk1, wk2, wv1, wv2)


def _t5_bucket(dist):
    n = jnp.maximum(dist, 0)
    nf = jnp.maximum(n, 1).astype(F32)
    large = MAX_EXACT + (jnp.log(nf / MAX_EXACT) / math.log(MAX_DISTANCE / MAX_EXACT)
                         * (N_BUCKETS - MAX_EXACT)).astype(jnp.int32)
    large = jnp.minimum(large, N_BUCKETS - 1)
    return jnp.where(n < MAX_EXACT, n, large)


def _bias_table_kernel(rb_ref, bucket_ref, o_ref):
    h = pl.program_id(0)
    bucket = bucket_ref[...]
    out = jnp.full(bucket.shape, MASKED, F32)
    for bkt in range(N_BUCKETS):
        out = jnp.where(bucket == bkt, rb_ref[bkt, h] * LOG2E, out)
    o_ref[0] = out


def _bias_table(rel_bias, bucket, *, tr):
    rows, lanes = bucket.shape
    return pl.pallas_call(
        _bias_table_kernel,
        out_shape=jax.ShapeDtypeStruct((N_HEADS, rows, lanes), F32),
        grid=(N_HEADS, rows // tr),
        in_specs=[pl.BlockSpec(memory_space=pltpu.SMEM), pl.BlockSpec((tr, lanes), lambda h, i: (i, 0))],
        out_specs=pl.BlockSpec((1, tr, lanes), lambda h, i: (h, i, 0)),
        compiler_params=_params(2),
        name="bias_table",
    )(rel_bias, bucket)


def _bias_tables(rel_bias, s):
    key = jnp.arange(LANES)[:, None]
    qry = jnp.arange(LANES)[None, :]

    def toeplitz_blocks(offsets, max_dist):
        dist = LANES * offsets[:, None, None] + qry[None] - key[None]
        valid = (dist >= 0) & (dist < max_dist)
        return jnp.where(valid, _t5_bucket(dist), -1).astype(jnp.int32).reshape(-1, LANES)

    sel_bucket = toeplitz_blocks(jnp.arange(-1, N_FAR + 1), 1 << 30)
    win_bucket = toeplitz_blocks(jnp.arange(-1, N_WIN_BLOCKS), WINDOW)
    offsets = jnp.arange(-1, N_CMP_FAR + 1)[:, None, None]
    cmp_dist = (LANES * offsets + jnp.arange(LANES)[None, None, :]
                - (jnp.arange(SUBLANES)[None, :, None] * CMP_STRIDE + CMP_LEN - 1))
    cmp_bucket = jnp.where(cmp_dist >= 0, _t5_bucket(cmp_dist), -1).astype(jnp.int32).reshape(-1, LANES)
    t_sel = _bias_table(rel_bias, sel_bucket, tr=sel_bucket.shape[0]).reshape(N_HEADS, N_FAR + 2, LANES, LANES)
    t_win = _bias_table(rel_bias, win_bucket, tr=win_bucket.shape[0]).reshape(N_HEADS, N_WIN_BLOCKS + 1, LANES, LANES)
    t_cmp = _bias_table(rel_bias, cmp_bucket, tr=cmp_bucket.shape[0]).reshape(N_HEADS, N_CMP_FAR + 2, SUBLANES, LANES)
    return t_sel, t_win, t_cmp


def _sublane_fold(x, op):
    parts = [x[k * SUBLANES:(k + 1) * SUBLANES] for k in range(x.shape[0] // SUBLANES)]
    while len(parts) > 1:
        parts = [op(parts[k], parts[k + 1]) for k in range(0, len(parts) - 1, 2)] + parts[len(parts) & ~1:]
    return parts[0]


def _cmp_kernel(qt_ref, kc_ref, vct_ref, tc_ref, ovt_ref, *rest, n_blk, k_sel, first_tile):
    oct_ref, mtt_ref = rest[-2:]
    heads = range(HEADS_PER_GROUP)
    q_t = jnp.concatenate([qt_ref[0, 0, r] for r in heads], axis=1)
    first_blk = (first_tile + pl.program_id(0)) * (CQ // LANES)
    bias = jnp.concatenate(
        [jnp.concatenate([tc_ref[r, jnp.clip(first_blk + lb - m, -1, N_CMP_FAR) + 1]
                          for r in heads for lb in range(CQ // LANES)], axis=1)
         for m in range(kc_ref.shape[2] // SUBLANES)], axis=0)
    s = _dot(kc_ref[0, 0], q_t) + bias
    m = jnp.maximum(jnp.max(s, axis=0, keepdims=True), M_INIT)
    e = jnp.exp2(s - m)
    pr = e * (1.0 / jnp.maximum(jnp.sum(e, axis=0, keepdims=True), 1e-30))
    o_t = _dot(vct_ref[0, 0], pr.astype(BF16))
    for r in heads:
        oct_ref[0, 0, r] = o_t[:, r * CQ:(r + 1) * CQ]

    psum = pr[:, 0:CQ]
    for r in heads[1:]:
        psum = psum + pr[:, r * CQ:(r + 1) * CQ]
    hi = psum.astype(BF16)
    rem = psum - hi.astype(F32)
    mid = rem.astype(BF16)
    lo = (rem - mid.astype(F32)).astype(BF16)
    ov_t = ovt_ref[...]
    imp = _dot(ov_t, hi) + _dot(ov_t, mid) + _dot(ov_t, lo)

    t = (first_tile + pl.program_id(0)) * CQ + lax.broadcasted_iota(jnp.int32, (LANES, CQ), 1)
    j = lax.broadcasted_iota(jnp.int32, (LANES, CQ), 0)
    cur = t // SEL_BLOCK
    forced = (j == 0) | (j == cur) | (j == cur - 1)
    score = jnp.where(forced, FORCE_SCORE, jnp.where(j * SEL_BLOCK <= t, imp, -1.0))

    terms = []
    for v in range(n_blk // SUBLANES):
        mine = score[v * SUBLANES:(v + 1) * SUBLANES]
        jj = v * SUBLANES + lax.broadcasted_iota(jnp.int32, (SUBLANES, CQ), 0)
        rank = jnp.zeros((SUBLANES, CQ), jnp.int32)
        for i in range(n_blk):
            other = score[i:i + 1]
            if i < v * SUBLANES:
                beats = other >= mine
            elif i >= (v + 1) * SUBLANES:
                beats = other > mine
            else:
                beats = (other > mine) | ((other == mine) & (i < jj))
            rank = rank + beats.astype(jnp.int32)
        terms.append(jnp.where(rank < k_sel, 0.0, MASK_TERM))
    if n_blk < SEL_BLOCK:
        terms.append(jnp.full((SEL_BLOCK - n_blk, CQ), MASK_TERM, F32))
    mtt_ref[0, 0] = jnp.concatenate(terms, axis=0).astype(BF16)


def _cmp_branch(q_t, kc, vc_t, t_cmp, overlap_t):
    b, _, _, _, s = q_t.shape
    nc = kc.shape[2]
    n_sel = s // SEL_BLOCK
    tiles_per_part = s // CQ // CMP_PARTS
    out_shape = (jax.ShapeDtypeStruct((b, N_GROUPS, HEADS_PER_GROUP, HEAD_DIM, s), F32),
                 jax.ShapeDtypeStruct((b, N_GROUPS, SEL_BLOCK, s), BF16))
    outs = ()
    for part in range(CMP_PARTS):
        first = part * tiles_per_part
        n_blk = n_sel * (part + 1) // CMP_PARTS
        nc_part = min(nc, -(-(nc * (part + 1) // CMP_PARTS) // LANES) * LANES)
        qspec = pl.BlockSpec((1, 1, HEADS_PER_GROUP, HEAD_DIM, CQ),
                             lambda qi, gi, bi, first=first: (bi, gi, 0, 0, first + qi))
        outs = pl.pallas_call(
            functools.partial(_cmp_kernel, n_blk=n_blk, k_sel=min(N_SELECT, n_sel), first_tile=first),
            out_shape=out_shape,
            grid=(tiles_per_part, N_GROUPS, b),
            in_specs=[qspec,
                      pl.BlockSpec((1, 1, nc_part, HEAD_DIM), lambda qi, gi, bi: (bi, gi, 0, 0)),
                      pl.BlockSpec((1, 1, HEAD_DIM, nc_part), lambda qi, gi, bi: (bi, gi, 0, 0)),
                      pl.BlockSpec((HEADS_PER_GROUP,) + t_cmp.shape[1:], lambda qi, gi, bi: (gi, 0, 0, 0)),
                      _resident((LANES, nc_part))] + [pl.BlockSpec(memory_space=pl.ANY)] * len(outs),
            out_specs=(qspec, pl.BlockSpec((1, 1, SEL_BLOCK, CQ),
                                           lambda qi, gi, bi, first=first: (bi, gi, 0, first + qi))),
            input_output_aliases={5 + k: k for k in range(len(outs))},
            compiler_params=_params(3),
            name="cmp_branch",
        )(q_t, kc, vc_t, t_cmp, overlap_t[:, :nc_part], *outs)
    return outs


def _key_tiles(tq):
    return (tq * QB + QB + KB - 1) // KB


def _selwin_kernel(qta_ref, qtb_ref, mta_ref, mtb_ref, ksa_ref, kwa_ref, vst_ref, vwt_ref, tsel_ref, twin_ref,
                   oca_ref, ocb_ref, gta_ref, gtb_ref, oa_ref, ob_ref,
                   qa_scr, part_scr, win_scr, wsc_scr, ssc_scr, *, n_q):
    i = pl.program_id(2)
    heads = range(HEADS_PER_GROUP)
    cols = HEADS_PER_GROUP * QT
    n_slots = part_scr.shape[0]
    q_tiles = (i, n_q - 1 - i)
    n_first = _key_tiles(i)

    for w, (qt_ref, mt_ref) in enumerate(((qta_ref, mta_ref), (qtb_ref, mtb_ref))):
        qa_scr[w] = jnp.concatenate(
            [jnp.concatenate([qt_ref[0, 0, r], mt_ref[0, 0]], axis=0) for r in heads], axis=1)

    def slot_info(slot):
        second = slot >= n_first
        w = second.astype(jnp.int32)
        kt = jnp.where(second, slot - n_first, slot)
        tq = jnp.where(second, q_tiles[1], q_tiles[0])
        return second, w, kt, tq

    def normalised(aug):
        return aug[:HEAD_DIM] * (1.0 / jnp.maximum(aug[HEAD_DIM:HEAD_DIM + 1], 1e-30))

    def win_first_block(w):
        return jnp.maximum(q_tiles[w] * QB - WINDOW // LANES, 0)

    def biased_scores(raw, table_ref, block_index, out_ref):
        folded = []
        for r in heads:
            for qb in range(QB):
                cs = slice(r * QT + qb * LANES, r * QT + (qb + 1) * LANES)
                fold = None
                for mb in range(raw.shape[0] // LANES):
                    rs = slice(mb * LANES, (mb + 1) * LANES)
                    sb = raw[rs, cs] + table_ref[r, block_index(mb, qb)]
                    out_ref[rs, cs] = sb
                    part = _sublane_fold(sb, jnp.maximum)
                    fold = part if fold is None else jnp.maximum(fold, part)
                folded.append(fold)
        return jnp.maximum(jnp.max(jnp.concatenate(folded, axis=1), axis=0, keepdims=True), M_INIT)

    def win_scores(w):
        first_blk = win_first_block(w)
        w0 = pl.multiple_of(first_blk * LANES, LANES)
        raw = _dot(kwa_ref[0, 0, pl.ds(w0, WIN_TILE_BLOCKS * LANES), :], qa_scr[w])

        def block_index(mb, qb):
            a = q_tiles[w] * QB + qb - (first_blk + mb)
            return jnp.where((a < 0) | (a >= N_WIN_BLOCKS), 0, a + 1)
        return biased_scores(raw, twin_ref, block_index, wsc_scr.at[w])

    def win_finish(w, m_w):
        first_blk = win_first_block(w)
        win = None
        for mb in range(WIN_TILE_BLOCKS):
            e_w = jnp.exp2(wsc_scr[w, mb * LANES:(mb + 1) * LANES, :] - m_w).astype(BF16)
            term = _dot(vwt_ref[0, 0, first_blk + mb], e_w)
            win = term if win is None else win + term
        win_scr[w] = normalised(win)

    m_slots = {}
    ring = ssc_scr.shape[0]

    def sel_scores(slot):
        second, w, kt, tq = slot_info(slot)
        k0 = pl.multiple_of(kt * KT, KT)
        ssc_scr[slot % ring] = _dot(ksa_ref[0, 0, pl.ds(k0, KT), :], qa_scr[w])

    def sel_finish(slot, _):
        second, w, kt, tq = slot_info(slot)
        block_index = lambda mb, qb: jnp.clip(tq * QB + qb - (kt * KB + mb), -1, N_FAR) + 1
        buf = ssc_scr.at[slot % ring]
        m_slot = m_slots[slot] = biased_scores(buf, tsel_ref, block_index, buf)
        p = jnp.exp2(ssc_scr[slot % ring] - m_slot).astype(BF16)
        part_scr[slot] = _dot(vst_ref[0, 0, kt], p)[:PART_ROWS]

    items = [(sel_scores, sel_finish, k) for k in range(n_slots)] + [(win_scores, win_finish, w) for w in range(2)]

    def merge_slots():
        max_first = _key_tiles(n_q // 2 - 1)
        seconds = [slot_info(slot)[0] for slot in range(n_slots)]
        m_fin = [functools.reduce(jnp.maximum, [jnp.where(seconds[k], M_INIT, m_slots[k]) for k in range(max_first)]),
                 functools.reduce(jnp.maximum, [jnp.where(seconds[k], m_slots[k], M_INIT) if k < max_first
                                                else m_slots[k] for k in range(1, n_slots)])]
        merged = []
        for w in range(2):
            acc = None
            for slot in (range(max_first) if w == 0 else range(1, n_slots)):
                weight = jnp.exp2(m_slots[slot] - m_fin[w])
                if 0 < slot < max_first:
                    owned = seconds[slot] if w == 1 else jnp.logical_not(seconds[slot])
                    weight = jnp.where(owned, weight, 0.0)
                term = part_scr[slot] * weight
                acc = term if acc is None else acc + term
            merged.append(normalised(acc))
        return merged

    pending = {}
    o_sels = None
    for n in range(len(items) + SCORE_LOOKAHEAD):
        if n < len(items):
            scores, _, arg = items[n]
            pending[n] = scores(arg)
        if n >= SCORE_LOOKAHEAD:
            _, finish, arg = items[n - SCORE_LOOKAHEAD]
            finish(arg, pending.pop(n - SCORE_LOOKAHEAD))
            if n - SCORE_LOOKAHEAD == n_slots - 1:
                o_sels = merge_slots()

    for w, (oc_ref, gt_ref, o_ref) in enumerate(((oca_ref, gta_ref, oa_ref), (ocb_ref, gtb_ref, ob_ref))):
        o_sel = o_sels[w]
        o_win = win_scr[w]

        gates = gt_ref[0, 0]
        mixed = []
        for r in heads:
            cs = slice(r * QT, (r + 1) * QT)
            g_c, g_s, g_w = (gates[3 * r + j:3 * r + j + 1] for j in range(3))
            mixed.append(g_c * oc_ref[0, 0, r] + g_s * o_sel[:, cs] + g_w * o_win[:, cs])
        pairs = [jnp.concatenate(mixed[k:k + 2], axis=0).T for k in range(0, HEADS_PER_GROUP, 2)]
        o_ref[0] = jnp.concatenate(pairs, axis=1).astype(BF16)


def _selwin_branch(q_t, mterm_t, ksa, kwa, vs_t, vw_t, t_sel, t_win, o_cmp_t, gates_t):
    b, _, _, _, s = q_t.shape
    n_q = s // QT
    half = n_q // 2
    n_slots = _key_tiles(0) + _key_tiles(n_q - 1)
    assert all(_key_tiles(i) + _key_tiles(n_q - 1 - i) == n_slots for i in range(half))
    cols = HEADS_PER_GROUP * QT
    fwd = lambda gi, bi, i: i
    bwd = lambda gi, bi, i: n_q - 1 - i
    qspec = lambda pick: pl.BlockSpec((1, 1, HEADS_PER_GROUP, HEAD_DIM, QT),
                                      lambda gi, bi, i: (bi, gi, 0, 0, pick(gi, bi, i)))
    rowspec = lambda rows, pick: pl.BlockSpec((1, 1, rows, QT), lambda gi, bi, i: (bi, gi, 0, pick(gi, bi, i)))
    whole = lambda arr: pl.BlockSpec((1, 1) + arr.shape[2:], lambda gi, bi, i: (bi, gi) + (0,) * (arr.ndim - 2))
    tab = lambda n: pl.BlockSpec((HEADS_PER_GROUP, n, LANES, LANES), lambda gi, bi, i: (gi, 0, 0, 0))
    out_shape = jax.ShapeDtypeStruct((b, s // 2, N_HEADS * HEAD_DIM), BF16)
    return pl.pallas_call(
        functools.partial(_selwin_kernel, n_q=n_q),
        out_shape=(out_shape, out_shape),
        grid=(N_GROUPS, b, half),
        in_specs=[qspec(fwd), qspec(bwd), rowspec(SEL_BLOCK, fwd), rowspec(SEL_BLOCK, bwd),
                  whole(ksa), whole(kwa), whole(vs_t), whole(vw_t),
                  tab(t_sel.shape[1]), tab(t_win.shape[1]),
                  qspec(fwd), qspec(bwd), rowspec(GATE_ROWS, fwd), rowspec(GATE_ROWS, bwd)],
        out_specs=(pl.BlockSpec((1, QT, GROUP_DIM), lambda gi, bi, i: (bi, i, gi)),
                   pl.BlockSpec((1, QT, GROUP_DIM), lambda gi, bi, i: (bi, half - 1 - i, gi))),
        scratch_shapes=[pltpu.VMEM((2, 2 * HEAD_DIM, cols), BF16),
                        pltpu.VMEM((n_slots, PART_ROWS, cols), F32),
                        pltpu.VMEM((2, HEAD_DIM, cols), F32),
                        pltpu.VMEM((2, WIN_TILE_BLOCKS * LANES, cols), F32),
                        pltpu.VMEM((SCORE_LOOKAHEAD + 1, KT, cols), F32)],
        compiler_params=_params(3),
        name="selwin_branch",
    )(q_t, q_t, mterm_t, mterm_t, ksa, kwa, vs_t, vw_t, t_sel, t_win, o_cmp_t, o_cmp_t, gates_t, gates_t)


def _nsa_mixer(x3, norm_g, w_in, pos_k, pos_v, wk1, wk2, wv1, wv2, tables):
    b, s, d = x3.shape
    n_q = s // QT
    assert n_q % 2 == 0 and s % CQ == 0 and s >= WIN_TILE_BLOCKS * LANES and s // SEL_BLOCK <= SEL_BLOCK
    t_sel, t_win, t_cmp = tables
    q_dim = N_HEADS * HEAD_DIM
    wq = w_in[:, :q_dim].astype(BF16)
    wkv = w_in[:, q_dim:q_dim + 6 * KV_DIM].astype(BF16)
    wg = w_in[:, q_dim + 6 * KV_DIM:].reshape(d, N_GROUPS, 3 * HEADS_PER_GROUP)
    wg = jnp.pad(wg, ((0, 0), (0, 0), (0, LANES - 3 * HEADS_PER_GROUP))).reshape(d, N_GROUPS * LANES).astype(BF16)
    q_t, kc_raw, vc_raw, ksa, kwa, vs_t, vw_t, gates_t = _nsa_in(x3, norm_g, wq, wkv, wg)

    nc = s // CMP_STRIDE
    per_row = lambda w: w.reshape(CMP_LEN // ROWS_PER_DOT, ROWS_PER_DOT * HEAD_DIM, -1).astype(BF16)
    widen = lambda w: jnp.pad(w, ((0, 0), (0, LANES - HEAD_DIM))).astype(BF16)
    kc, vc_t = _compress(kc_raw, vc_raw, pos_k, pos_v, per_row(wk1), widen(wk2), per_row(wv1), widen(wv2))

    n_cmp = (s - CMP_LEN) // CMP_STRIDE + 1
    c_start = jnp.arange(nc)[None, :] * CMP_STRIDE
    n_start = jnp.arange(LANES)[:, None] * SEL_BLOCK
    overlap_t = ((c_start < n_start + SEL_BLOCK) & (c_start + CMP_LEN > n_start)
                 & (jnp.arange(nc)[None, :] < n_cmp) & (jnp.arange(LANES)[:, None] < s // SEL_BLOCK))
    o_cmp_t, mterm_t = _cmp_branch(q_t, kc, vc_t, t_cmp, overlap_t.astype(BF16))
    return _selwin_branch(q_t, mterm_t, ksa, kwa, vs_t, vw_t, t_sel, t_win, o_cmp_t, gates_t)


def kernel(x, p, rel_bias, ffn1_norm, ffn1_w_gate, ffn1_w_up, ffn1_w_down, mix_norm, ffn2_norm, ffn2_w_gate, ffn2_w_up, ffn2_w_down, ple_norm, ple_w_gate, ple_w_in, conv_w_pw1, conv_b_pw1, conv_w_dw, conv_b_dw, conv_ln_g, conv_ln_b, conv_w_pw2, conv_b_pw2, nsa_w_in, nsa_cmp_pos_k, nsa_cmp_pos_v, nsa_cmp_wk1, nsa_cmp_wk2, nsa_cmp_wv1, nsa_cmp_wv2, nsa_w_out, final_norm):
    b, s, d = x.shape
    depth = ffn1_norm.shape[0]
    row = lambda v: v.reshape(1, -1)
    tables = _bias_tables(rel_bias, s)
    bf = lambda w: w.astype(BF16)
    for i in range(depth):
        j = i // 2
        ffn1 = (row(ffn1_norm[i]), bf(ffn1_w_gate[i]), bf(ffn1_w_up[i]), bf(ffn1_w_down[i]))
        ffn2 = (row(ffn2_norm[i]), bf(ffn2_w_gate[i]), bf(ffn2_w_up[i]), bf(ffn2_w_down[i]))
        ple = ((p, i), row(ple_norm[i]), bf(ple_w_gate[i]), bf(ple_w_in[i]))
        if i == depth - 1:
            ple += (row(final_norm),)
        if i % 2 == 0:
            x, u = _ffn(x, *ffn1, glu=(row(mix_norm[i]), bf(conv_w_pw1[j]), row(conv_b_pw1[j])))
            x = _conv_out(u, x, conv_w_dw[j], row(conv_b_dw[j]), row(conv_ln_g[j]), row(conv_ln_b[j]),
                          bf(conv_w_pw2[j]), row(conv_b_pw2[j]))
            x = _ffn(x, *ffn2, ple=ple)
        else:
            x = _ffn(x, *ffn1)
            o_lo, o_hi = _nsa_mixer(x, row(mix_norm[i]), nsa_w_in[j], nsa_cmp_pos_k[j], nsa_cmp_pos_v[j],
                                    nsa_cmp_wk1[j], nsa_cmp_wk2[j], nsa_cmp_wv1[j], nsa_cmp_wv2[j], tables)
            x = _ffn(x, *ffn2, proj=(o_lo, o_hi, bf(nsa_w_out[j])), ple=ple)
    return x
```

```python
import functools
import math

import jax
import jax.numpy as jnp
from jax import lax
from jax.experimental import pallas as pl
from jax.experimental.pallas import tpu as pltpu

F32 = jnp.float32
BF16 = jnp.bfloat16

D_MODEL = 1024
D_FF = 2816
CONV_KERNEL = 31
HEAD_DIM = 64
N_HEADS = 16
N_GROUPS = 4
HEADS_PER_GROUP = 4
GROUP_DIM = HEADS_PER_GROUP * HEAD_DIM
KV_DIM = N_GROUPS * HEAD_DIM
CMP_LEN = 32
CMP_STRIDE = 16
SEL_BLOCK = 64
N_SELECT = 16
WINDOW = 512
N_BUCKETS = 32
MAX_EXACT = 16
MAX_DISTANCE = 2048
RMS_EPS = 1e-6
LN_EPS = 1e-5
FORCE_SCORE = 1e9

LANES = 128
SUBLANES = 8
QT = 128
KT = 256
QB = QT // LANES
KB = KT // LANES
CQ = 512
CMP_PARTS = 4
ROWS_PER_DOT = 4
GATE_ROWS = 16
SCORE_LOOKAHEAD = 2
HALO = 32
MASKED = -1e30
MASK_TERM = -(2.0 ** 100)
M_INIT = -1e20
LOG2E = math.log2(math.e)
V_ROWS = HEAD_DIM + 16
PART_ROWS = HEAD_DIM + SUBLANES
N_FAR = MAX_DISTANCE // LANES + 1
N_CMP_FAR = -(-(MAX_DISTANCE + (SUBLANES - 1) * CMP_STRIDE + CMP_LEN - 1) // LANES)
N_WIN_BLOCKS = WINDOW // LANES + 1
WIN_TILE_BLOCKS = WINDOW // LANES + QB
VMEM_LIMIT = 56 * 1024 * 1024


def _params(n_axes):
    return pltpu.CompilerParams(dimension_semantics=("arbitrary",) * n_axes,
                                vmem_limit_bytes=VMEM_LIMIT)


def _resident(shape):
    nd = len(shape)
    return pl.BlockSpec(shape, lambda *_: (0,) * nd, pipeline_mode=pl.Buffered(1))


def _rms(x, g):
    return x * lax.rsqrt(jnp.mean(x * x, axis=-1, keepdims=True) + RMS_EPS) * g


def _dot(a, b):
    return jnp.dot(a, b, preferred_element_type=F32)


def _ffn_kernel(*refs, row_parts, half_tiles, epilogue):
    refs = list(refs)
    take = lambda n: [refs.pop(0) for _ in range(n)]
    (x_ref,) = take(1)
    proj = take(3) if half_tiles is not None else None
    g_ref, wg_ref, wu_ref, wd_ref = take(4)
    extra = take({"none": 0, "ple": 4, "ple_final": 5, "glu": 3}[epilogue])
    (o_ref,) = take(1)
    u_ref = take(1)[0] if epilogue == "glu" else None

    rows_per_part = x_ref.shape[1] // row_parts
    parts = [slice(k * rows_per_part, (k + 1) * rows_per_part) for k in range(row_parts)]
    hidden = []
    for rows in parts:
        x = x_ref[0, rows, :]
        if proj is not None:
            lo_ref, hi_ref, wo_ref = proj
            o = jnp.where(pl.program_id(1) < half_tiles, lo_ref[0, rows, :], hi_ref[0, rows, :])
            x = x + _dot(o, wo_ref[...])
            o_ref[0, rows, :] = x
        xn = _rms(x, g_ref[...]).astype(BF16)
        gate = _dot(xn, wg_ref[...])
        up = _dot(xn, wu_ref[...])
        hidden.append(((gate * jax.nn.sigmoid(gate)) * up).astype(BF16))
    for rows, h in zip(parts, hidden):
        x = x_ref[0, rows, :] if proj is None else o_ref[0, rows, :]
        y = x + 0.5 * _dot(h, wd_ref[...])
        if epilogue in ("ple", "ple_final"):
            p_ref, pg_ref, pwg_ref, pwi_ref = extra[:4]
            gate = jax.nn.sigmoid(_dot(_rms(y, pg_ref[...]).astype(BF16), pwg_ref[...]))
            y = y + gate * _dot(p_ref[0, rows, :].astype(BF16), pwi_ref[...])
            if epilogue == "ple_final":
                y = _rms(y, extra[4][...])
        elif epilogue == "glu":
            cg_ref, cw_ref, cb_ref = extra
            d = y.shape[-1]
            hn = _rms(y, cg_ref[...]).astype(BF16)
            a = _dot(hn, cw_ref[:, :d]) + cb_ref[:, :d]
            gt = _dot(hn, cw_ref[:, d:]) + cb_ref[:, d:]
            u_ref[0, rows, :] = a * jax.nn.sigmoid(gt)
        o_ref[0, rows, :] = y


def _ffn(x3, norm_g, wg, wu, wd, *, proj=None, ple=None, glu=None):
    b, s, d = x3.shape
    tm, row_parts = (512, 2) if proj is not None else (1024, 4)
    tile = pl.BlockSpec((1, tm, d), lambda bi, i: (bi, i, 0))
    operands, in_specs = [x3], [tile]
    half_tiles = None
    if proj is not None:
        o_lo, o_hi, w_out = proj
        half_tiles = s // 2 // tm
        width = o_lo.shape[2]
        operands += [o_lo, o_hi, w_out]
        in_specs += [pl.BlockSpec((1, tm, width), lambda bi, i: (bi, jnp.minimum(i, half_tiles - 1), 0)),
                     pl.BlockSpec((1, tm, width), lambda bi, i: (bi, jnp.maximum(i - half_tiles, 0), 0)),
                     _resident(w_out.shape)]
    weights = [norm_g, wg, wu, wd]
    epilogue = "none"
    out_shape = [jax.ShapeDtypeStruct((b, s, d), F32)]
    if ple is not None:
        epilogue = "ple" if len(ple) == 4 else "ple_final"
        p_all, layer = ple[0]
        operands_after = [p_all]
        specs_after = [pl.BlockSpec((None, 1, tm, p_all.shape[3]), lambda bi, i: (layer, bi, i, 0))]
        weights_after = list(ple[1:])
    elif glu is not None:
        epilogue = "glu"
        operands_after, specs_after, weights_after = [], [], list(glu)
        out_shape.append(jax.ShapeDtypeStruct((b, s, d), F32))
    else:
        operands_after, specs_after, weights_after = [], [], []
    operands += weights + operands_after + weights_after
    in_specs += [_resident(w.shape) for w in weights] + specs_after + [_resident(w.shape) for w in weights_after]
    out = pl.pallas_call(
        functools.partial(_ffn_kernel, row_parts=row_parts, half_tiles=half_tiles, epilogue=epilogue),
        out_shape=tuple(out_shape),
        grid=(b, s // tm),
        in_specs=in_specs,
        out_specs=tuple([tile] * len(out_shape)),
        compiler_params=_params(2),
        name="ffn",
    )(*operands)
    return out if glu is not None else out[0]


def _conv_out_kernel(u_ref, halo_ref, x_ref, wdw_ref, bdw_ref, lng_ref, lnb_ref, w2_ref, b2_ref,
                     o_ref, pad_ref, conv_ref, *, ts):
    first = pl.program_id(1) == 0
    pad_ref[0, 0:HALO, :] = jnp.where(first, 0.0, halo_ref[0])
    pad_ref[0, HALO:, :] = u_ref[0]
    padded = pad_ref[0]
    for ph in range(1, SUBLANES):
        pad_ref[ph] = pltpu.roll(padded, ts + HALO - ph, 0)
    off = HALO - (CONV_KERNEL - 1)
    d = u_ref.shape[-1]
    for c in range(d // LANES):
        cs = slice(c * LANES, (c + 1) * LANES)
        acc = jnp.zeros((ts, LANES), F32)
        for k in range(CONV_KERNEL):
            ph = (off + k) % SUBLANES
            base = off + k - ph
            acc = acc + wdw_ref[k:k + 1, cs] * pad_ref[ph, base:base + ts, cs]
        conv_ref[:, cs] = acc + bdw_ref[:, cs]
    u = conv_ref[...]
    mu = jnp.mean(u, axis=-1, keepdims=True)
    var = jnp.mean(jnp.square(u - mu), axis=-1, keepdims=True)
    y = (u - mu) * lax.rsqrt(var + LN_EPS) * lng_ref[...] + lnb_ref[...]
    y = y * jax.nn.sigmoid(y)
    o_ref[0] = x_ref[0] + _dot(y.astype(BF16), w2_ref[...]) + b2_ref[...]


def _conv_out(u3, x3, w_dw, b_dw, ln_g, ln_b, w_pw2, b_pw2, *, ts=256):
    b, s, d = u3.shape
    tile = pl.BlockSpec((1, ts, d), lambda bi, i: (bi, i, 0))
    halo = pl.BlockSpec((1, HALO, d), lambda bi, i: (bi, jnp.maximum(i * (ts // HALO) - 1, 0), 0))
    return pl.pallas_call(
        functools.partial(_conv_out_kernel, ts=ts),
        out_shape=jax.ShapeDtypeStruct((b, s, d), F32),
        grid=(b, s // ts),
        in_specs=[tile, halo, tile, _resident((CONV_KERNEL, d)), _resident((1, d)), _resident((1, d)),
                  _resident((1, d)), _resident((d, d)), _resident((1, d))],
        out_specs=tile,
        scratch_shapes=[pltpu.VMEM((SUBLANES, ts + HALO, d), F32), pltpu.VMEM((ts, d), F32)],
        compiler_params=_params(2),
        name="conv_out",
    )(u3, u3, x3, w_dw, b_dw, ln_g, ln_b, w_pw2, b_pw2)


def _nsa_in_kernel(x_ref, g_ref, wq_ref, wkv_ref, wgt_ref,
                   qt_ref, kc_ref, vc_ref, ksa_ref, kwa_ref, vst_ref, vwt_ref, gt_ref, *, tm):
    hn = _rms(x_ref[0], g_ref[...]).astype(BF16)
    q = _dot(hn, wq_ref[...]) * (HEAD_DIM ** -0.5 * LOG2E)
    heads_per_chunk = LANES // HEAD_DIM

    def transposed_heads(cols, chunk):
        t = cols[:, chunk * LANES:(chunk + 1) * LANES].T
        return [t[hh * HEAD_DIM:(hh + 1) * HEAD_DIM] for hh in range(heads_per_chunk)]

    for chunk in range(N_HEADS // heads_per_chunk):
        for hh, qt in enumerate(transposed_heads(q, chunk)):
            g, r = divmod(chunk * heads_per_chunk + hh, HEADS_PER_GROUP)
            qt_ref[0, g, r] = qt.astype(BF16)

    gates = jax.nn.sigmoid(_dot(hn, wgt_ref[...]))
    for g in range(N_GROUPS):
        gt_ref[0, g] = gates[:, g * LANES:(g + 1) * LANES].T[:GATE_ROWS]

    kv = _dot(hn, wkv_ref[...])
    pos = pl.program_id(1) * tm + lax.broadcasted_iota(jnp.int32, (tm, SEL_BLOCK), 0)
    blk = lax.broadcasted_iota(jnp.int32, (tm, SEL_BLOCK), 1)
    onehot = jnp.where(pos // SEL_BLOCK == blk, 1.0, 0.0).astype(BF16)
    nothing = jnp.zeros((tm, SEL_BLOCK), BF16)
    for g in range(N_GROUPS):
        def part(j):
            lo = j * KV_DIM + g * HEAD_DIM
            return kv[:, lo:lo + HEAD_DIM]
        kc_ref[0, g] = part(0)
        vc_ref[0, g] = part(1)
        ksa_ref[0, g] = jnp.concatenate([part(2).astype(BF16), onehot], axis=1)
        kwa_ref[0, g] = jnp.concatenate([part(4).astype(BF16), nothing], axis=1)
    v_sel = kv[:, 3 * KV_DIM:4 * KV_DIM]
    v_win = kv[:, 5 * KV_DIM:6 * KV_DIM]
    for chunk in range(N_GROUPS // heads_per_chunk):
        for hh, (vs, vw) in enumerate(zip(transposed_heads(v_sel, chunk), transposed_heads(v_win, chunk))):
            g = chunk * heads_per_chunk + hh
            for j in range(tm // KT):
                vst_ref[0, g, j, 0:HEAD_DIM, :] = vs[:, j * KT:(j + 1) * KT].astype(BF16)
                vst_ref[0, g, j, HEAD_DIM:, :] = jnp.ones((V_ROWS - HEAD_DIM, KT), BF16)
            for j in range(tm // LANES):
                vwt_ref[0, g, j, 0:HEAD_DIM, :] = vw[:, j * LANES:(j + 1) * LANES].astype(BF16)
                vwt_ref[0, g, j, HEAD_DIM:, :] = jnp.ones((V_ROWS - HEAD_DIM, LANES), BF16)


def _nsa_in(x3, norm_g, wq, wkv, wgt, *, tm=512):
    b, s, d = x3.shape
    per_group = lambda width: pl.BlockSpec((1, N_GROUPS, tm, width), lambda bi, i: (bi, 0, i, 0))
    kv_shape = lambda width, dt: jax.ShapeDtypeStruct((b, N_GROUPS, s, width), dt)
    return pl.pallas_call(
        functools.partial(_nsa_in_kernel, tm=tm),
        out_shape=(jax.ShapeDtypeStruct((b, N_GROUPS, HEADS_PER_GROUP, HEAD_DIM, s), BF16),
                   kv_shape(HEAD_DIM, F32), kv_shape(HEAD_DIM, F32),
                   kv_shape(2 * HEAD_DIM, BF16), kv_shape(2 * HEAD_DIM, BF16),
                   jax.ShapeDtypeStruct((b, N_GROUPS, s // KT, V_ROWS, KT), BF16),
                   jax.ShapeDtypeStruct((b, N_GROUPS, s // LANES, V_ROWS, LANES), BF16),
                   jax.ShapeDtypeStruct((b, N_GROUPS, GATE_ROWS, s), F32)),
        grid=(b, s // tm),
        in_specs=[pl.BlockSpec((1, tm, d), lambda bi, i: (bi, i, 0)), _resident((1, d)),
                  _resident(wq.shape), _resident(wkv.shape), _resident(wgt.shape)],
        out_specs=(pl.BlockSpec((1, N_GROUPS, HEADS_PER_GROUP, HEAD_DIM, tm), lambda bi, i: (bi, 0, 0, 0, i)),
                   per_group(HEAD_DIM), per_group(HEAD_DIM), per_group(2 * HEAD_DIM), per_group(2 * HEAD_DIM),
                   pl.BlockSpec((1, N_GROUPS, tm // KT, V_ROWS, KT), lambda bi, i: (bi, 0, i, 0, 0)),
                   pl.BlockSpec((1, N_GROUPS, tm // LANES, V_ROWS, LANES), lambda bi, i: (bi, 0, i, 0, 0)),
                   pl.BlockSpec((1, N_GROUPS, GATE_ROWS, tm), lambda bi, i: (bi, 0, 0, i))),
        compiler_params=_params(2),
        name="nsa_in",
    )(x3, norm_g, wq, wkv, wgt)


def _compress_kernel(kx_ref, vx_ref, pk_ref, pv_ref, wk1_ref, wk2_ref, wv1_ref, wv2_ref, kc_ref, vct_ref, *, nc):
    def mlp(x_ref, p_ref, w1_ref, w2_ref):
        first = second = None
        for l0 in range(0, CMP_STRIDE, ROWS_PER_DOT):
            rows = [x_ref[0, 0, pl.ds(l, nc, stride=CMP_STRIDE), :] for l in range(l0, l0 + ROWS_PER_DOT)]

            def shifted(offset):
                return jnp.concatenate([(r + p_ref[offset + l:offset + l + 1, :]).astype(BF16)
                                        for l, r in zip(range(l0, l0 + ROWS_PER_DOT), rows)], axis=1)
            lo = _dot(shifted(0), w1_ref[l0 // ROWS_PER_DOT])
            hi = _dot(shifted(CMP_STRIDE), w1_ref[(CMP_STRIDE + l0) // ROWS_PER_DOT])
            first = lo if first is None else first + lo
            second = hi if second is None else second + hi
        hidden = first + pltpu.roll(second, nc - 1, 0)
        return _dot(jax.nn.gelu(hidden).astype(BF16), w2_ref[...])
    kc_ref[0, 0] = mlp(kx_ref, pk_ref, wk1_ref, wk2_ref)[:, :HEAD_DIM].astype(BF16)
    vct_ref[0, 0] = mlp(vx_ref, pv_ref, wv1_ref, wv2_ref).T[:HEAD_DIM].astype(BF16)


def _compress(kx, vx, pos_k, pos_v, wk1, wk2, wv1, wv2):
    b, g, s, width = kx.shape
    nc = s // CMP_STRIDE
    blk = pl.BlockSpec((1, 1, s, width), lambda bi, gi: (bi, gi, 0, 0))
    return pl.pallas_call(
        functools.partial(_compress_kernel, nc=nc),
        out_shape=(jax.ShapeDtypeStruct((b, g, nc, HEAD_DIM), BF16),
                   jax.ShapeDtypeStruct((b, g, HEAD_DIM, nc), BF16)),
        grid=(b, g),
        in_specs=[blk, blk, _resident(pos_k.shape), _resident(pos_v.shape), _resident(wk1.shape),
                  _resident(wk2.shape), _resident(wv1.shape), _resident(wv2.shape)],
        out_specs=(pl.BlockSpec((1, 1, nc, HEAD_DIM), lambda bi, gi: (bi, gi, 0, 0)),
                   pl.BlockSpec((1, 1, HEAD_DIM, nc), lambda bi, gi: (bi, gi, 0, 0))),
        compiler_params=_params(2),
        name="compress",
    )(kx, vx, pos_k, pos_v, wk1, wk2, wv1, wv2)


def _t5_bucket(dist):
    n = jnp.maximum(dist, 0)
    nf = jnp.maximum(n, 1).astype(F32)
    large = MAX_EXACT + (jnp.log(nf / MAX_EXACT) / math.log(MAX_DISTANCE / MAX_EXACT)
                         * (N_BUCKETS - MAX_EXACT)).astype(jnp.int32)
    large = jnp.minimum(large, N_BUCKETS - 1)
    return jnp.where(n < MAX_EXACT, n, large)


def _bias_table_kernel(rb_ref, bucket_ref, o_ref):
    h = pl.program_id(0)
    bucket = bucket_ref[...]
    out = jnp.full(bucket.shape, MASKED, F32)
    for bkt in range(N_BUCKETS):
        out = jnp.where(bucket == bkt, rb_ref[bkt, h] * LOG2E, out)
    o_ref[0] = out


def _bias_table(rel_bias, bucket, *, tr):
    rows, lanes = bucket.shape
    return pl.pallas_call(
        _bias_table_kernel,
        out_shape=jax.ShapeDtypeStruct((N_HEADS, rows, lanes), F32),
        grid=(N_HEADS, rows // tr),
        in_specs=[pl.BlockSpec(memory_space=pltpu.SMEM), pl.BlockSpec((tr, lanes), lambda h, i: (i, 0))],
        out_specs=pl.BlockSpec((1, tr, lanes), lambda h, i: (h, i, 0)),
        compiler_params=_params(2),
        name="bias_table",
    )(rel_bias, bucket)


def _bias_tables(rel_bias, s):
    key = jnp.arange(LANES)[:, None]
    qry = jnp.arange(LANES)[None, :]

    def toeplitz_blocks(offsets, max_dist):
        dist = LANES * offsets[:, None, None] + qry[None] - key[None]
        valid = (dist >= 0) & (dist < max_dist)
        return jnp.where(valid, _t5_bucket(dist), -1).astype(jnp.int32).reshape(-1, LANES)

    sel_bucket = toeplitz_blocks(jnp.arange(-1, N_FAR + 1), 1 << 30)
    win_bucket = toeplitz_blocks(jnp.arange(-1, N_WIN_BLOCKS), WINDOW)
    offsets = jnp.arange(-1, N_CMP_FAR + 1)[:, None, None]
    cmp_dist = (LANES * offsets + jnp.arange(LANES)[None, None, :]
                - (jnp.arange(SUBLANES)[None, :, None] * CMP_STRIDE + CMP_LEN - 1))
    cmp_bucket = jnp.where(cmp_dist >= 0, _t5_bucket(cmp_dist), -1).astype(jnp.int32).reshape(-1, LANES)
    t_sel = _bias_table(rel_bias, sel_bucket, tr=sel_bucket.shape[0]).reshape(N_HEADS, N_FAR + 2, LANES, LANES)
    t_win = _bias_table(rel_bias, win_bucket, tr=win_bucket.shape[0]).reshape(N_HEADS, N_WIN_BLOCKS + 1, LANES, LANES)
    t_cmp = _bias_table(rel_bias, cmp_bucket, tr=cmp_bucket.shape[0]).reshape(N_HEADS, N_CMP_FAR + 2, SUBLANES, LANES)
    return t_sel, t_win, t_cmp


def _sublane_fold(x, op):
    parts = [x[k * SUBLANES:(k + 1) * SUBLANES] for k in range(x.shape[0] // SUBLANES)]
    while len(parts) > 1:
        parts = [op(parts[k], parts[k + 1]) for k in range(0, len(parts) - 1, 2)] + parts[len(parts) & ~1:]
    return parts[0]


def _cmp_kernel(qt_ref, kc_ref, vct_ref, tc_ref, ovt_ref, *rest, n_blk, k_sel, first_tile):
    oct_ref, mtt_ref = rest[-2:]
    heads = range(HEADS_PER_GROUP)
    q_t = jnp.concatenate([qt_ref[0, 0, r] for r in heads], axis=1)
    first_blk = (first_tile + pl.program_id(0)) * (CQ // LANES)
    bias = jnp.concatenate(
        [jnp.concatenate([tc_ref[r, jnp.clip(first_blk + lb - m, -1, N_CMP_FAR) + 1]
                          for r in heads for lb in range(CQ // LANES)], axis=1)
         for m in range(kc_ref.shape[2] // SUBLANES)], axis=0)
    s = _dot(kc_ref[0, 0], q_t) + bias
    m = jnp.maximum(jnp.max(s, axis=0, keepdims=True), M_INIT)
    e = jnp.exp2(s - m)
    pr = e * (1.0 / jnp.maximum(jnp.sum(e, axis=0, keepdims=True), 1e-30))
    o_t = _dot(vct_ref[0, 0], pr.astype(BF16))
    for r in heads:
        oct_ref[0, 0, r] = o_t[:, r * CQ:(r + 1) * CQ]

    psum = pr[:, 0:CQ]
    for r in heads[1:]:
        psum = psum + pr[:, r * CQ:(r + 1) * CQ]
    hi = psum.astype(BF16)
    rem = psum - hi.astype(F32)
    mid = rem.astype(BF16)
    lo = (rem - mid.astype(F32)).astype(BF16)
    ov_t = ovt_ref[...]
    imp = _dot(ov_t, hi) + _dot(ov_t, mid) + _dot(ov_t, lo)

    t = (first_tile + pl.program_id(0)) * CQ + lax.broadcasted_iota(jnp.int32, (LANES, CQ), 1)
    j = lax.broadcasted_iota(jnp.int32, (LANES, CQ), 0)
    cur = t // SEL_BLOCK
    forced = (j == 0) | (j == cur) | (j == cur - 1)
    score = jnp.where(forced, FORCE_SCORE, jnp.where(j * SEL_BLOCK <= t, imp, -1.0))

    terms = []
    for v in range(n_blk // SUBLANES):
        mine = score[v * SUBLANES:(v + 1) * SUBLANES]
        jj = v * SUBLANES + lax.broadcasted_iota(jnp.int32, (SUBLANES, CQ), 0)
        rank = jnp.zeros((SUBLANES, CQ), jnp.int32)
        for i in range(n_blk):
            other = score[i:i + 1]
            if i < v * SUBLANES:
                beats = other >= mine
            elif i >= (v + 1) * SUBLANES:
                beats = other > mine
            else:
                beats = (other > mine) | ((other == mine) & (i < jj))
            rank = rank + beats.astype(jnp.int32)
        terms.append(jnp.where(rank < k_sel, 0.0, MASK_TERM))
    if n_blk < SEL_BLOCK:
        terms.append(jnp.full((SEL_BLOCK - n_blk, CQ), MASK_TERM, F32))
    mtt_ref[0, 0] = jnp.concatenate(terms, axis=0).astype(BF16)


def _cmp_branch(q_t, kc, vc_t, t_cmp, overlap_t):
    b, _, _, _, s = q_t.shape
    nc = kc.shape[2]
    n_sel = s // SEL_BLOCK
    tiles_per_part = s // CQ // CMP_PARTS
    out_shape = (jax.ShapeDtypeStruct((b, N_GROUPS, HEADS_PER_GROUP, HEAD_DIM, s), F32),
                 jax.ShapeDtypeStruct((b, N_GROUPS, SEL_BLOCK, s), BF16))
    outs = ()
    for part in range(CMP_PARTS):
        first = part * tiles_per_part
        n_blk = n_sel * (part + 1) // CMP_PARTS
        nc_part = min(nc, -(-(nc * (part + 1) // CMP_PARTS) // LANES) * LANES)
        qspec = pl.BlockSpec((1, 1, HEADS_PER_GROUP, HEAD_DIM, CQ),
                             lambda qi, gi, bi, first=first: (bi, gi, 0, 0, first + qi))
        outs = pl.pallas_call(
            functools.partial(_cmp_kernel, n_blk=n_blk, k_sel=min(N_SELECT, n_sel), first_tile=first),
            out_shape=out_shape,
            grid=(tiles_per_part, N_GROUPS, b),
            in_specs=[qspec,
                      pl.BlockSpec((1, 1, nc_part, HEAD_DIM), lambda qi, gi, bi: (bi, gi, 0, 0)),
                      pl.BlockSpec((1, 1, HEAD_DIM, nc_part), lambda qi, gi, bi: (bi, gi, 0, 0)),
                      pl.BlockSpec((HEADS_PER_GROUP,) + t_cmp.shape[1:], lambda qi, gi, bi: (gi, 0, 0, 0)),
                      _resident((LANES, nc_part))] + [pl.BlockSpec(memory_space=pl.ANY)] * len(outs),
            out_specs=(qspec, pl.BlockSpec((1, 1, SEL_BLOCK, CQ),
                                           lambda qi, gi, bi, first=first: (bi, gi, 0, first + qi))),
            input_output_aliases={5 + k: k for k in range(len(outs))},
            compiler_params=_params(3),
            name="cmp_branch",
        )(q_t, kc, vc_t, t_cmp, overlap_t[:, :nc_part], *outs)
    return outs


def _key_tiles(tq):
    return (tq * QB + QB + KB - 1) // KB


def _selwin_kernel(qta_ref, qtb_ref, mta_ref, mtb_ref, ksa_ref, kwa_ref, vst_ref, vwt_ref, tsel_ref, twin_ref,
                   oca_ref, ocb_ref, gta_ref, gtb_ref, oa_ref, ob_ref,
                   qa_scr, part_scr, win_scr, wsc_scr, ssc_scr, *, n_q):
    i = pl.program_id(2)
    heads = range(HEADS_PER_GROUP)
    cols = HEADS_PER_GROUP * QT
    n_slots = part_scr.shape[0]
    q_tiles = (i, n_q - 1 - i)
    n_first = _key_tiles(i)

    for w, (qt_ref, mt_ref) in enumerate(((qta_ref, mta_ref), (qtb_ref, mtb_ref))):
        qa_scr[w] = jnp.concatenate(
            [jnp.concatenate([qt_ref[0, 0, r], mt_ref[0, 0]], axis=0) for r in heads], axis=1)

    def slot_info(slot):
        second = slot >= n_first
        w = second.astype(jnp.int32)
        kt = jnp.where(second, slot - n_first, slot)
        tq = jnp.where(second, q_tiles[1], q_tiles[0])
        return second, w, kt, tq

    def normalised(aug):
        return aug[:HEAD_DIM] * (1.0 / jnp.maximum(aug[HEAD_DIM:HEAD_DIM + 1], 1e-30))

    def win_first_block(w):
        return jnp.maximum(q_tiles[w] * QB - WINDOW // LANES, 0)

    def biased_scores(raw, table_ref, block_index, out_ref):
        folded = []
        for r in heads:
            for qb in range(QB):
                cs = slice(r * QT + qb * LANES, r * QT + (qb + 1) * LANES)
                fold = None
                for mb in range(raw.shape[0] // LANES):
                    rs = slice(mb * LANES, (mb + 1) * LANES)
                    sb = raw[rs, cs] + table_ref[r, block_index(mb, qb)]
                    out_ref[rs, cs] = sb
                    part = _sublane_fold(sb, jnp.maximum)
                    fold = part if fold is None else jnp.maximum(fold, part)
                folded.append(fold)
        return jnp.maximum(jnp.max(jnp.concatenate(folded, axis=1), axis=0, keepdims=True), M_INIT)

    def win_scores(w):
        first_blk = win_first_block(w)
        w0 = pl.multiple_of(first_blk * LANES, LANES)
        raw = _dot(kwa_ref[0, 0, pl.ds(w0, WIN_TILE_BLOCKS * LANES), :], qa_scr[w])

        def block_index(mb, qb):
            a = q_tiles[w] * QB + qb - (first_blk + mb)
            return jnp.where((a < 0) | (a >= N_WIN_BLOCKS), 0, a + 1)
        return biased_scores(raw, twin_ref, block_index, wsc_scr.at[w])

    def win_finish(w, m_w):
        first_blk = win_first_block(w)
        win = None
        for mb in range(WIN_TILE_BLOCKS):
            e_w = jnp.exp2(wsc_scr[w, mb * LANES:(mb + 1) * LANES, :] - m_w).astype(BF16)
            term = _dot(vwt_ref[0, 0, first_blk + mb], e_w)
            win = term if win is None else win + term
        win_scr[w] = normalised(win)

    m_slots = {}
    ring = ssc_scr.shape[0]

    def sel_scores(slot):
        second, w, kt, tq = slot_info(slot)
        k0 = pl.multiple_of(kt * KT, KT)
        ssc_scr[slot % ring] = _dot(ksa_ref[0, 0, pl.ds(k0, KT), :], qa_scr[w])

    def sel_finish(slot, _):
        second, w, kt, tq = slot_info(slot)
        block_index = lambda mb, qb: jnp.clip(tq * QB + qb - (kt * KB + mb), -1, N_FAR) + 1
        buf = ssc_scr.at[slot % ring]
        m_slot = m_slots[slot] = biased_scores(buf, tsel_ref, block_index, buf)
        p = jnp.exp2(ssc_scr[slot % ring] - m_slot).astype(BF16)
        part_scr[slot] = _dot(vst_ref[0, 0, kt], p)[:PART_ROWS]

    items = [(sel_scores, sel_finish, k) for k in range(n_slots)] + [(win_scores, win_finish, w) for w in range(2)]

    def merge_slots():
        max_first = _key_tiles(n_q // 2 - 1)
        seconds = [slot_info(slot)[0] for slot in range(n_slots)]
        m_fin = [functools.reduce(jnp.maximum, [jnp.where(seconds[k], M_INIT, m_slots[k]) for k in range(max_first)]),
                 functools.reduce(jnp.maximum, [jnp.where(seconds[k], m_slots[k], M_INIT) if k < max_first
                                                else m_slots[k] for k in range(1, n_slots)])]
        merged = []
        for w in range(2):
            acc = None
            for slot in (range(max_first) if w == 0 else range(1, n_slots)):
                weight = jnp.exp2(m_slots[slot] - m_fin[w])
                if 0 < slot < max_first:
                    owned = seconds[slot] if w == 1 else jnp.logical_not(seconds[slot])
                    weight = jnp.where(owned, weight, 0.0)
                term = part_scr[slot] * weight
                acc = term if acc is None else acc + term
            merged.append(normalised(acc))
        return merged

    pending = {}
    o_sels = None
    for n in range(len(items) + SCORE_LOOKAHEAD):
        if n < len(items):
            scores, _, arg = items[n]
            pending[n] = scores(arg)
        if n >= SCORE_LOOKAHEAD:
            _, finish, arg = items[n - SCORE_LOOKAHEAD]
            finish(arg, pending.pop(n - SCORE_LOOKAHEAD))
            if n - SCORE_LOOKAHEAD == n_slots - 1:
                o_sels = merge_slots()

    for w, (oc_ref, gt_ref, o_ref) in enumerate(((oca_ref, gta_ref, oa_ref), (ocb_ref, gtb_ref, ob_ref))):
        o_sel = o_sels[w]
        o_win = win_scr[w]

        gates = gt_ref[0, 0]
        mixed = []
        for r in heads:
            cs = slice(r * QT, (r + 1) * QT)
            g_c, g_s, g_w = (gates[3 * r + j:3 * r + j + 1] for j in range(3))
            mixed.append(g_c * oc_ref[0, 0, r] + g_s * o_sel[:, cs] + g_w * o_win[:, cs])
        pairs = [jnp.concatenate(mixed[k:k + 2], axis=0).T for k in range(0, HEADS_PER_GROUP, 2)]
        o_ref[0] = jnp.concatenate(pairs, axis=1).astype(BF16)


def _selwin_branch(q_t, mterm_t, ksa, kwa, vs_t, vw_t, t_sel, t_win, o_cmp_t, gates_t):
    b, _, _, _, s = q_t.shape
    n_q = s // QT
    half = n_q // 2
    n_slots = _key_tiles(0) + _key_tiles(n_q - 1)
    assert all(_key_tiles(i) + _key_tiles(n_q - 1 - i) == n_slots for i in range(half))
    cols = HEADS_PER_GROUP * QT
    fwd = lambda gi, bi, i: i
    bwd = lambda gi, bi, i: n_q - 1 - i
    qspec = lambda pick: pl.BlockSpec((1, 1, HEADS_PER_GROUP, HEAD_DIM, QT),
                                      lambda gi, bi, i: (bi, gi, 0, 0, pick(gi, bi, i)))
    rowspec = lambda rows, pick: pl.BlockSpec((1, 1, rows, QT), lambda gi, bi, i: (bi, gi, 0, pick(gi, bi, i)))
    whole = lambda arr: pl.BlockSpec((1, 1) + arr.shape[2:], lambda gi, bi, i: (bi, gi) + (0,) * (arr.ndim - 2))
    tab = lambda n: pl.BlockSpec((HEADS_PER_GROUP, n, LANES, LANES), lambda gi, bi, i: (gi, 0, 0, 0))
    out_shape = jax.ShapeDtypeStruct((b, s // 2, N_HEADS * HEAD_DIM), BF16)
    return pl.pallas_call(
        functools.partial(_selwin_kernel, n_q=n_q),
        out_shape=(out_shape, out_shape),
        grid=(N_GROUPS, b, half),
        in_specs=[qspec(fwd), qspec(bwd), rowspec(SEL_BLOCK, fwd), rowspec(SEL_BLOCK, bwd),
                  whole(ksa), whole(kwa), whole(vs_t), whole(vw_t),
                  tab(t_sel.shape[1]), tab(t_win.shape[1]),
                  qspec(fwd), qspec(bwd), rowspec(GATE_ROWS, fwd), rowspec(GATE_ROWS, bwd)],
        out_specs=(pl.BlockSpec((1, QT, GROUP_DIM), lambda gi, bi, i: (bi, i, gi)),
                   pl.BlockSpec((1, QT, GROUP_DIM), lambda gi, bi, i: (bi, half - 1 - i, gi))),
        scratch_shapes=[pltpu.VMEM((2, 2 * HEAD_DIM, cols), BF16),
                        pltpu.VMEM((n_slots, PART_ROWS, cols), F32),
                        pltpu.VMEM((2, HEAD_DIM, cols), F32),
                        pltpu.VMEM((2, WIN_TILE_BLOCKS * LANES, cols), F32),
                        pltpu.VMEM((SCORE_LOOKAHEAD + 1, KT, cols), F32)],
        compiler_params=_params(3),
        name="selwin_branch",
    )(q_t, q_t, mterm_t, mterm_t, ksa, kwa, vs_t, vw_t, t_sel, t_win, o_cmp_t, o_cmp_t, gates_t, gates_t)


def _nsa_mixer(x3, norm_g, w_in, pos_k, pos_v, wk1, wk2, wv1, wv2, tables):
    b, s, d = x3.shape
    n_q = s // QT
    assert n_q % 2 == 0 and s % CQ == 0 and s >= WIN_TILE_BLOCKS * LANES and s // SEL_BLOCK <= SEL_BLOCK
    t_sel, t_win, t_cmp = tables
    q_dim = N_HEADS * HEAD_DIM
    wq = w_in[:, :q_dim].astype(BF16)
    wkv = w_in[:, q_dim:q_dim + 6 * KV_DIM].astype(BF16)
    wg = w_in[:, q_dim + 6 * KV_DIM:].reshape(d, N_GROUPS, 3 * HEADS_PER_GROUP)
    wg = jnp.pad(wg, ((0, 0), (0, 0), (0, LANES - 3 * HEADS_PER_GROUP))).reshape(d, N_GROUPS * LANES).astype(BF16)
    q_t, kc_raw, vc_raw, ksa, kwa, vs_t, vw_t, gates_t = _nsa_in(x3, norm_g, wq, wkv, wg)

    nc = s // CMP_STRIDE
    per_row = lambda w: w.reshape(CMP_LEN // ROWS_PER_DOT, ROWS_PER_DOT * HEAD_DIM, -1).astype(BF16)
    widen = lambda w: jnp.pad(w, ((0, 0), (0, LANES - HEAD_DIM))).astype(BF16)
    kc, vc_t = _compress(kc_raw, vc_raw, pos_k, pos_v, per_row(wk1), widen(wk2), per_row(wv1), widen(wv2))

    n_cmp = (s - CMP_LEN) // CMP_STRIDE + 1
    c_start = jnp.arange(nc)[None, :] * CMP_STRIDE
    n_start = jnp.arange(LANES)[:, None] * SEL_BLOCK
    overlap_t = ((c_start < n_start + SEL_BLOCK) & (c_start + CMP_LEN > n_start)
                 & (jnp.arange(nc)[None, :] < n_cmp) & (jnp.arange(LANES)[:, None] < s // SEL_BLOCK))
    o_cmp_t, mterm_t = _cmp_branch(q_t, kc, vc_t, t_cmp, overlap_t.astype(BF16))
    return _selwin_branch(q_t, mterm_t, ksa, kwa, vs_t, vw_t, t_sel, t_win, o_cmp_t, gates_t)


def kernel(x, p, rel_bias, ffn1_norm, ffn1_w_gate, ffn1_w_up, ffn1_w_down, mix_norm, ffn2_norm, ffn2_w_gate, ffn2_w_up, ffn2_w_down, ple_norm, ple_w_gate, ple_w_in, conv_w_pw1, conv_b_pw1, conv_w_dw, conv_b_dw, conv_ln_g, conv_ln_b, conv_w_pw2, conv_b_pw2, nsa_w_in, nsa_cmp_pos_k, nsa_cmp_pos_v, nsa_cmp_wk1, nsa_cmp_wk2, nsa_cmp_wv1, nsa_cmp_wv2, nsa_w_out, final_norm):
    b, s, d = x.shape
    depth = ffn1_norm.shape[0]
    row = lambda v: v.reshape(1, -1)
    tables = _bias_tables(rel_bias, s)
    bf = lambda w: w.astype(BF16)
    for i in range(depth):
        j = i // 2
        ffn1 = (row(ffn1_norm[i]), bf(ffn1_w_gate[i]), bf(ffn1_w_up[i]), bf(ffn1_w_down[i]))
        ffn2 = (row(ffn2_norm[i]), bf(ffn2_w_gate[i]), bf(ffn2_w_up[i]), bf(ffn2_w_down[i]))
        ple = ((p, i), row(ple_norm[i]), bf(ple_w_gate[i]), bf(ple_w_in[i]))
        if i == depth - 1:
            ple += (row(final_norm),)
        if i % 2 == 0:
            x, u = _ffn(x, *ffn1, glu=(row(mix_norm[i]), bf(conv_w_pw1[j]), row(conv_b_pw1[j])))
            x = _conv_out(u, x, conv_w_dw[j], row(conv_b_dw[j]), row(conv_ln_g[j]), row(conv_ln_b[j]),
                          bf(conv_w_pw2[j]), row(conv_b_pw2[j]))
            x = _ffn(x, *ffn2, ple=ple)
        else:
            x = _ffn(x, *ffn1)
            o_lo, o_hi = _nsa_mixer(x, row(mix_norm[i]), nsa_w_in[j], nsa_cmp_pos_k[j], nsa_cmp_pos_v[j],
                                    nsa_cmp_wk1[j], nsa_cmp_wk2[j], nsa_cmp_wv1[j], nsa_cmp_wv2[j], tables)
            x = _ffn(x, *ffn2, proj=(o_lo, o_hi, bf(nsa_w_out[j])), ple=ple)
    return x
```
